```python
import math
import jax, jax.numpy as jnp
from jax import lax
import numpy as np

D_MODEL = 2048
BATCH = 4
SEQ = 2048
DEPTH = 4

N_A = DEPTH // 2
N_B = DEPTH - N_A
HGRN_HEADS = D_MODEL // 128
HGRN_DK = 128
HGRN_DV = D_MODEL // HGRN_HEADS
HGRN_CHUNK = 32
ATTN_HEAD_DIM = 128
ATTN_HEADS = D_MODEL // ATTN_HEAD_DIM
MOBA_BLOCK = 256
MOBA_TOPK = 3
MOBA_QUERY_BLOCK = 4
D_FF = 4 * D_MODEL
EPS = 1e-6
NEG_BIG = -1e30
LB_FLOOR = 1e-30

kernel_name = "hgrn2_moba_yoco_hybrid"

F32 = jnp.float32


def rms_norm(x, g):
    xf = x.astype(F32)
    return xf * lax.rsqrt(jnp.mean(xf * xf, axis=-1, keepdims=True) + EPS) * g.astype(F32)


def chunked_gla(q, k, v, log_f):
    B, S, H, DK = q.shape
    DV = v.shape[-1]
    C = HGRN_CHUNK
    NC = S // C

    def to_chunks(t):
        return t.reshape(B, NC, C, H, t.shape[-1]).transpose(1, 0, 3, 2, 4)

    causal = jnp.tril(jnp.ones((C, C), dtype=bool))[:, :, None]

    def step(state, inp):
        qc, kc, vc, gc = inp
        b = jnp.cumsum(gc, axis=-2)
        inter = jnp.einsum('bhtd,bhde->bhte', qc * jnp.exp(b), state)
        diff = b[:, :, :, None, :] - b[:, :, None, :, :]
        decay = jnp.where(causal, jnp.exp(jnp.where(causal, diff, 0.0)), 0.0)
        scores = jnp.einsum('bhtd,bhsd,bhtsd->bhts', qc, kc, decay)
        intra = jnp.einsum('bhts,bhse->bhte', scores, vc)
        b_last = b[:, :, -1]
        new_state = jnp.exp(b_last)[..., None] * state + jnp.einsum(
            'bhsd,bhse->bhde', kc * jnp.exp(b_last[:, :, None] - b), vc)
        return new_state, inter + intra

    state0 = jnp.zeros((B, H, DK, DV), F32)
    _, out = lax.scan(step, state0, (to_chunks(q), to_chunks(k), to_chunks(v), to_chunks(log_f)))
    return out.transpose(1, 0, 3, 2, 4).reshape(B, S, H, DV)


def hgrn2_mixer(hn, w_in, lb, head_gain, w_out):
    B, S, _ = hn.shape
    proj = hn @ w_in
    q, f_raw, i, g = jnp.split(proj.astype(F32), 4, axis=-1)
    q = jax.nn.silu(q)
    lb = lb.astype(F32)
    log_f = jnp.logaddexp(jnp.log(jnp.maximum(lb, LB_FLOOR)), jnp.log1p(-lb) + jax.nn.log_sigmoid(f_raw))
    k = (1.0 - lb) * jax.nn.sigmoid(-f_raw)
    heads = lambda t, d: t.reshape(B, S, HGRN_HEADS, d)
    o = chunked_gla(heads(q, HGRN_DK), heads(k, HGRN_DK), heads(i, HGRN_DV), heads(log_f, HGRN_DK))
    o = rms_norm(o, head_gain.reshape(HGRN_HEADS, HGRN_DV)).reshape(B, S, D_MODEL)
    o = o * jax.nn.silu(g)
    return o @ w_out


def shared_kv(h, kv_norm, w_kv, k_norm):
    B, S, _ = h.shape
    hn = rms_norm(h, kv_norm)
    k, v = jnp.split(hn @ w_kv, 2, axis=-1)
    k = rms_norm(k.reshape(B, S, ATTN_HEADS, ATTN_HEAD_DIM), k_norm)
    v = v.reshape(B, S, ATTN_HEADS, ATTN_HEAD_DIM).astype(F32)
    NB = -(-S // MOBA_BLOCK)
    pad = NB * MOBA_BLOCK - S

    def blocks(t):
        t = jnp.pad(t.transpose(0, 2, 1, 3), ((0, 0), (0, 0), (0, pad), (0, 0)))
        return t.reshape(B, ATTN_HEADS, NB, MOBA_BLOCK, ATTN_HEAD_DIM)

    k_blocks, v_blocks = blocks(k), blocks(v)
    counts = jnp.clip(S - jnp.arange(NB) * MOBA_BLOCK, 1, MOBA_BLOCK).astype(F32)
    k_mean = jnp.sum(k_blocks, axis=3) / counts[:, None]
    return k_blocks, v_blocks, k_mean


def moba_mixer(hn, w_q, q_norm, k_blocks, v_blocks, k_mean, w_o):
    B, S, _ = hn.shape
    H, dh = ATTN_HEADS, ATTN_HEAD_DIM
    NB = k_blocks.shape[2]
    q = rms_norm((hn @ w_q).reshape(B, S, H, dh), q_norm).transpose(0, 2, 1, 3)
    pos = jnp.arange(S)
    qblk = pos // MOBA_BLOCK
    gate = jnp.einsum('bhtd,bhnd->bhtn', q, k_mean)
    past = jnp.arange(NB)[None, :] < qblk[:, None]
    gate = jnp.where(past, gate, NEG_BIG)
    _, sel = lax.top_k(gate, min(MOBA_TOPK, NB))
    valid = sel < qblk[:, None]
    own = jnp.broadcast_to(qblk[:, None], (B, H, S, 1))
    sel = jnp.concatenate([sel, own], axis=-1).astype(jnp.int32)
    valid = jnp.concatenate([valid, jnp.ones(own.shape, dtype=bool)], axis=-1)

    scale = dh ** -0.5
    slopes = 2.0 ** (-8.0 * jnp.arange(1, H + 1, dtype=F32) / H)
    bi = jnp.arange(B)[:, None, None, None]
    hi = jnp.arange(H)[None, :, None, None]
    key_off = jnp.arange(MOBA_BLOCK)

    def attend(args):
        qc, selc, validc, tq = args
        kg = k_blocks[bi, hi, selc]
        vg = v_blocks[bi, hi, selc]
        kpos = selc[..., None] * MOBA_BLOCK + key_off
        dist = (tq[:, None, None] - kpos).astype(F32)
        logits = jnp.einsum('bhqd,bhqrkd->bhqrk', qc, kg) * scale - slopes[:, None, None, None] * dist
        mask = validc[..., None] & (dist >= 0)
        logits = jnp.where(mask, logits, NEG_BIG)
        p = jax.nn.softmax(logits.reshape(B, H, qc.shape[2], -1), axis=-1).reshape(logits.shape)
        p = jnp.where(mask, p, 0.0)
        return jnp.einsum('bhqrk,bhqrkd->bhqd', p, vg)

    NQ = S // MOBA_QUERY_BLOCK

    def qblocks(t):
        return jnp.moveaxis(t.reshape(B, H, NQ, MOBA_QUERY_BLOCK, *t.shape[3:]), 2, 0)

    out = lax.map(attend, (qblocks(q), qblocks(sel), qblocks(valid), pos.reshape(NQ, MOBA_QUERY_BLOCK)))
    out = jnp.moveaxis(out, 0, 2).reshape(B, H, S, dh).transpose(0, 2, 1, 3).reshape(B, S, D_MODEL)
    return out @ w_o


def sqrelu_mlp(hn, w1, w2):
    return jnp.square(jax.nn.relu(hn @ w1)) @ w2


def setup_inputs(seed: int = 0) -> dict:
    key = jax.random.key(seed)
    ks = jax.random.split(key, 20)
    nrm = lambda k, shape, fan_in: jax.random.normal(k, shape, F32) * (fan_in ** -0.5)
    gain = lambda k, shape: 1.0 + 0.02 * jax.random.normal(k, shape, F32)
    D = D_MODEL
    return {
        "x": jax.random.normal(ks[0], (BATCH, SEQ, D), F32),
        "a_norm": gain(ks[1], (N_A, D)),
        "a_w_in": nrm(ks[2], (N_A, D, 4 * D), D),
        "a_head_norm": gain(ks[3], (N_A, D)),
        "a_w_out": nrm(ks[4], (N_A, D, D), D),
        "lower_bounds": 0.1 * jax.random.normal(ks[5], (N_A, D), F32),
        "kv_norm": gain(ks[6], (D,)),
        "w_kv": nrm(ks[7], (D, 2 * D), D),
        "k_norm": gain(ks[8], (ATTN_HEAD_DIM,)),
        "b_norm": gain(ks[9], (N_B, D)),
        "b_w_q": nrm(ks[10], (N_B, D, D), D),
        "b_q_norm": gain(ks[11], (N_B, ATTN_HEAD_DIM)),
        "b_w_o": nrm(ks[12], (N_B, D, D), D),
        "mlp_norm": gain(ks[13], (DEPTH, D)),
        "mlp_w1": nrm(ks[14], (DEPTH, D, D_FF), D),
        "mlp_w2": nrm(ks[15], (DEPTH, D_FF, D), D_FF),
    }


def reference(x, a_norm, a_w_in, a_head_norm, a_w_out, lower_bounds, kv_norm, w_kv, k_norm,
              b_norm, b_w_q, b_q_norm, b_w_o, mlp_norm, mlp_w1, mlp_w2):
    h = x
    p = jax.nn.softmax(lower_bounds.astype(F32), axis=0)
    lbs = jnp.cumsum(p, axis=0) - p[0]
    k_blocks = v_blocks = k_mean = None
    for l in range(DEPTH):
        if l < N_A:
            mix = hgrn2_mixer(rms_norm(h, a_norm[l]), a_w_in[l], lbs[l], a_head_norm[l], a_w_out[l])
        else:
            if l == N_A:
                k_blocks, v_blocks, k_mean = shared_kv(h, kv_norm, w_kv, k_norm)
            j = l - N_A
            mix = moba_mixer(rms_norm(h, b_norm[j]), b_w_q[j], b_q_norm[j], k_blocks, v_blocks, k_mean, b_w_o[j])
        h = h + mix.astype(h.dtype)
        h = h + sqrelu_mlp(rms_norm(h, mlp_norm[l]), mlp_w1[l], mlp_w2[l]).astype(h.dtype)
    return h
```

```python
import functools

import numpy as np
import jax
import jax.numpy as jnp
from jax import lax
from jax.experimental import pallas as pl
from jax.experimental.pallas import tpu as pltpu

F32 = jnp.float32
BF16 = jnp.bfloat16

HEAD_DIM = 128
MOBA_BLOCK = 256
MOBA_TOPK = 3
GLA_CHUNK = 64
EPS = 1e-6
NEG_BIG = -1e30
LB_FLOOR = 1e-30

V7X_VMEM_LIMIT_BYTES = 56 * 1024 * 1024


def _params(*sem):
    return pltpu.CompilerParams(dimension_semantics=sem, vmem_limit_bytes=V7X_VMEM_LIMIT_BYTES)


def _dot(a, b):
    return jnp.dot(a, b, preferred_element_type=F32)


def _dot_nt(a, b):
    return lax.dot_general(a, b, (((1,), (1,)), ((), ())), preferred_element_type=F32)


def _dot_tn(a, b):
    return lax.dot_general(a, b, (((0,), (0,)), ((), ())), preferred_element_type=F32)


def _rms(x, g):
    return x * lax.rsqrt(jnp.mean(x * x, axis=-1, keepdims=True) + EPS) * g


def _silu(x):
    return x * jax.nn.sigmoid(x)


def _log_sigmoid(x):
    return jnp.minimum(x, 0.0) - jnp.log1p(jnp.exp(-jnp.abs(x)))


def _logaddexp(a, b):
    return jnp.maximum(a, b) + jnp.log1p(jnp.exp(-jnp.abs(a - b)))


def _head_rms(x, g):
    outs = []
    for hh in range(x.shape[1] // HEAD_DIM):
        outs.append(_rms(x[:, hh * HEAD_DIM:(hh + 1) * HEAD_DIM], g))
    return outs


def _hgrn_in_kernel(layer, h_ref, ng_ref, lb_ref, wq_ref, wf_ref, wi_ref, wg_ref,
                    q_ref, lf_ref, k_ref, v_ref, gt_ref, xn_ref):
    @pl.when(pl.program_id(1) == 0)
    def _():
        xn_ref[...] = _rms(h_ref[...], ng_ref[...]).astype(BF16)

    xn = xn_ref[...]
    pq = _dot(xn, wq_ref[...])
    q_ref[...] = _silu(pq).astype(BF16)

    lbr = lb_ref[...]
    rows = [lbr[r:r + 1, :] for r in range(lbr.shape[0])]
    mx = functools.reduce(jnp.maximum, rows)
    ex = [jnp.exp(r - mx) for r in rows]
    den = functools.reduce(lambda a, b: a + b, ex)
    p = [e / den for e in ex]
    lb = functools.reduce(lambda a, b: a + b, p[:layer + 1]) - p[0]

    pf = _dot(xn, wf_ref[...])
    lf_ref[...] = _logaddexp(jnp.log(jnp.maximum(lb, LB_FLOOR)),
                             jnp.log1p(-lb) + _log_sigmoid(pf))
    k_ref[...] = ((1.0 - lb) * jax.nn.sigmoid(-pf)).astype(BF16)

    v_ref[...] = _dot(xn, wi_ref[...]).astype(BF16)
    gt_ref[...] = _silu(_dot(xn, wg_ref[...])).astype(BF16)


def _hgrn_in(h, norm_g, lower_bounds, w_in, layer, tm, tn):
    m, d = h.shape
    nj = d // tn
    wspec = lambda g: pl.BlockSpec((d, tn), lambda i, j, g=g: (0, j + g * nj))
    ospec = pl.BlockSpec((tm, tn), lambda i, j: (i, j))
    return pl.pallas_call(
        functools.partial(_hgrn_in_kernel, layer),
        grid=(m // tm, nj),
        in_specs=[pl.BlockSpec((tm, d), lambda i, j: (i, 0)),
                  pl.BlockSpec((1, d), lambda i, j: (0, 0)),
                  pl.BlockSpec((lower_bounds.shape[0], tn), lambda i, j: (0, j)),
                  wspec(0), wspec(1), wspec(2), wspec(3)],
        out_specs=[ospec] * 5,
        out_shape=[jax.ShapeDtypeStruct((m, d), BF16), jax.ShapeDtypeStruct((m, d), F32),
                   jax.ShapeDtypeStruct((m, d), BF16), jax.ShapeDtypeStruct((m, d), BF16),
                   jax.ShapeDtypeStruct((m, d), BF16)],
        scratch_shapes=[pltpu.VMEM((tm, d), BF16)],
        compiler_params=_params("arbitrary", "arbitrary"),
        name="hgrn_in",
    )(h, norm_g.reshape(1, d), lower_bounds, w_in, w_in, w_in, w_in)


def _gla_tables(c):
    levels = int(np.log2(c))
    assert 1 << levels == c
    r = np.arange(c)
    mats = []
    lev = np.full((c, c), -1, np.int32)
    for l in range(levels):
        half = c >> (l + 1)
        seg, pos = r // (2 * half), r % (2 * half)
        mid = seg * 2 * half + half
        j = r[None, :]
        second = (pos >= half)[:, None]
        d = np.where(second, (j >= mid[:, None]) & (j <= r[:, None]),
                     (j > r[:, None]) & (j <= mid[:, None] - 1))
        mats.append(d)
        same = seg[:, None] == seg[None, :]
        lev[same & second & (pos < half)[None, :]] = l
    lev[r[:, None] == r[None, :]] = levels
    mats.append(r[None, :] <= r[:, None])
    mats.append(r[None, :] > r[:, None])
    return np.concatenate(mats, 0).astype(np.float32), lev, levels


def _gla_kernel(chunk, levels, q_ref, k_ref, v_ref, g_ref, gt_ref, hg_ref, dall_ref, lev_ref,
                o_ref, st_ref):
    @pl.when(pl.program_id(2) == 0)
    def _():
        st_ref[...] = jnp.zeros_like(st_ref)

    c = chunk
    dall = dall_ref[...]
    lev = lev_ref[...]
    hg = hg_ref[...]
    for ci in range(q_ref.shape[0] // c):
        sl = pl.ds(ci * c, c)
        g = g_ref[sl, :]
        g_hi = g.astype(BF16)
        r1 = g - g_hi.astype(F32)
        g_mid = r1.astype(BF16)
        g_lo = (r1 - g_mid.astype(F32)).astype(BF16)
        w = jnp.exp(_dot(dall, g_hi) + _dot(dall, g_mid) + _dot(dall, g_lo))
        q = q_ref[sl, :].astype(F32)
        k = k_ref[sl, :].astype(F32)
        v = v_ref[sl, :]
        scores = jnp.where(lev == levels, _dot_nt(q.astype(BF16), k.astype(BF16)), 0.0)
        for l in range(levels):
            wl = w[l * c:(l + 1) * c]
            scores = scores + jnp.where(
                lev == l, _dot_nt((q * wl).astype(BF16), (k * wl).astype(BF16)), 0.0)
        wb = w[levels * c:(levels + 1) * c]
        we = w[(levels + 1) * c:(levels + 2) * c]
        st = st_ref[...]
        o = _dot_nt((q * wb).astype(BF16), st.astype(BF16)) + _dot(scores.astype(BF16), v)
        st_ref[...] = st * wb[c - 1:c, :] + _dot_tn(v, (k * we).astype(BF16))
        o_ref[sl, :] = (_rms(o, hg) * gt_ref[sl, :].astype(F32)).astype(BF16)


def _gla(q, k, v, logf, gate, head_gain, batch, seq, ts):
    m, d = q.shape
    heads = d // HEAD_DIM
    ns = seq // ts
    dall, lev, levels = _gla_tables(GLA_CHUNK)
    spec = pl.BlockSpec((ts, HEAD_DIM), lambda b, h, s: (b * ns + s, h))
    return pl.pallas_call(
        functools.partial(_gla_kernel, GLA_CHUNK, levels),
        grid=(batch, heads, ns),
        in_specs=[spec, spec, spec, spec, spec,
                  pl.BlockSpec((1, HEAD_DIM), lambda b, h, s: (0, h)),
                  pl.BlockSpec(dall.shape, lambda b, h, s: (0, 0)),
                  pl.BlockSpec(lev.shape, lambda b, h, s: (0, 0))],
        out_specs=spec,
        out_shape=jax.ShapeDtypeStruct((m, d), BF16),
        scratch_shapes=[pltpu.VMEM((HEAD_DIM, HEAD_DIM), F32)],
        compiler_params=_params("arbitrary", "arbitrary", "arbitrary"),
        name="gla",
    )(q, k, v, logf, gate, head_gain.reshape(1, d), jnp.asarray(dall, BF16), jnp.asarray(lev))


def _proj_res_kernel(h_ref, a_ref, w_ref, o_ref):
    o_ref[...] = h_ref[...] + _dot(a_ref[...], w_ref[...])


def _proj_res(h, a, w, tm, tn):
    m, d = h.shape
    kdim = a.shape[1]
    return pl.pallas_call(
        _proj_res_kernel,
        grid=(m // tm, d // tn),
        in_specs=[pl.BlockSpec((tm, tn), lambda i, j: (i, j)),
                  pl.BlockSpec((tm, kdim), lambda i, j: (i, 0)),
                  pl.BlockSpec((kdim, tn), lambda i, j: (0, j))],
        out_specs=pl.BlockSpec((tm, tn), lambda i, j: (i, j)),
        out_shape=jax.ShapeDtypeStruct((m, d), F32),
        compiler_params=_params("arbitrary", "arbitrary"),
        name="proj_res",
    )(h, a, w)


def _mlp_kernel(h_ref, ng_ref, w1_ref, w2_ref, o_ref, xn_ref):
    @pl.when(pl.program_id(1) == 0)
    def _():
        x = h_ref[...]
        xn_ref[...] = _rms(x, ng_ref[...]).astype(BF16)
        o_ref[...] = x

    t = jnp.square(jnp.maximum(_dot(xn_ref[...], w1_ref[...]), 0.0)).astype(BF16)
    o_ref[...] += _dot(t, w2_ref[...])


def _mlp(h, norm_g, w1, w2, tm, tf):
    m, d = h.shape
    ff = w1.shape[1]
    return pl.pallas_call(
        _mlp_kernel,
        grid=(m // tm, ff // tf),
        in_specs=[pl.BlockSpec((tm, d), lambda i, j: (i, 0)),
                  pl.BlockSpec((1, d), lambda i, j: (0, 0)),
                  pl.BlockSpec((d, tf), lambda i, j: (0, j)),
                  pl.BlockSpec((tf, d), lambda i, j: (j, 0))],
        out_specs=pl.BlockSpec((tm, d), lambda i, j: (i, 0)),
        out_shape=jax.ShapeDtypeStruct((m, d), F32),
        scratch_shapes=[pltpu.VMEM((tm, d), BF16)],
        compiler_params=_params("arbitrary", "arbitrary"),
        name="mlp",
    )(h, norm_g.reshape(1, d), w1, w2)


def _kv_kernel(h_ref, ng_ref, kn_ref, wk_ref, wv_ref, k_ref, v_ref, km_ref, xn_ref):
    @pl.when(pl.program_id(1) == 0)
    def _():
        xn_ref[...] = _rms(h_ref[...], ng_ref[...]).astype(BF16)

    xn = xn_ref[...]
    v_ref[...] = _dot(xn, wv_ref[...]).astype(BF16)
    heads = _head_rms(_dot(xn, wk_ref[...]), kn_ref[...])
    nblk = km_ref.shape[1]
    for hh, kh in enumerate(heads):
        cols = slice(hh * HEAD_DIM, (hh + 1) * HEAD_DIM)
        k_ref[:, cols] = kh.astype(BF16)
        km_ref[0, :, cols] = jnp.sum(kh.reshape(nblk, MOBA_BLOCK, HEAD_DIM), axis=1) * (1.0 / MOBA_BLOCK)


def _kv(h, norm_g, k_norm, w_kv, tm, tn):
    m, d = h.shape
    nj = d // tn
    nblk = tm // MOBA_BLOCK
    return pl.pallas_call(
        _kv_kernel,
        grid=(m // tm, nj),
        in_specs=[pl.BlockSpec((tm, d), lambda i, j: (i, 0)),
                  pl.BlockSpec((1, d), lambda i, j: (0, 0)),
                  pl.BlockSpec((1, HEAD_DIM), lambda i, j: (0, 0)),
                  pl.BlockSpec((d, tn), lambda i, j: (0, j)),
                  pl.BlockSpec((d, tn), lambda i, j: (0, j + nj))],
        out_specs=[pl.BlockSpec((tm, tn), lambda i, j: (i, j)),
                   pl.BlockSpec((tm, tn), lambda i, j: (i, j)),
                   pl.BlockSpec((1, nblk, tn), lambda i, j: (i, 0, j))],
        out_shape=[jax.ShapeDtypeStruct((m, d), BF16), jax.ShapeDtypeStruct((m, d), BF16),
                   jax.ShapeDtypeStruct((m // tm, nblk, d), F32)],
        scratch_shapes=[pltpu.VMEM((tm, d), BF16)],
        compiler_params=_params("arbitrary", "arbitrary"),
        name="shared_kv",
    )(h, norm_g.reshape(1, d), k_norm.reshape(1, HEAD_DIM), w_kv, w_kv)


def _qproj_kernel(h_ref, ng_ref, qn_ref, w_ref, q_ref, xn_ref):
    @pl.when(pl.program_id(1) == 0)
    def _():
        xn_ref[...] = _rms(h_ref[...], ng_ref[...]).astype(BF16)

    for hh, qh in enumerate(_head_rms(_dot(xn_ref[...], w_ref[...]), qn_ref[...])):
        q_ref[:, hh * HEAD_DIM:(hh + 1) * HEAD_DIM] = qh.astype(BF16)


def _qproj(h, norm_g, q_norm, w_q, tm, tn):
    m, d = h.shape
    return pl.pallas_call(
        _qproj_kernel,
        grid=(m // tm, d // tn),
        in_specs=[pl.BlockSpec((tm, d), lambda i, j: (i, 0)),
                  pl.BlockSpec((1, d), lambda i, j: (0, 0)),
                  pl.BlockSpec((1, HEAD_DIM), lambda i, j: (0, 0)),
                  pl.BlockSpec((d, tn), lambda i, j: (0, j))],
        out_specs=pl.BlockSpec((tm, tn), lambda i, j: (i, j)),
        out_shape=jax.ShapeDtypeStruct((m, d), BF16),
        scratch_shapes=[pltpu.VMEM((tm, d), BF16)],
        compiler_params=_params("arbitrary", "arbitrary"),
        name="moba_q",
    )(h, norm_g.reshape(1, d), q_norm.reshape(1, HEAD_DIM), w_q)


def _moba_kernel(nblk, q_ref, k_ref, v_ref, km_ref, slope_ref, o_ref, vt_ref, sel_ref):
    blk = MOBA_BLOCK
    scale = HEAD_DIM ** -0.5
    for n in range(nblk):
        vt_ref[n] = v_ref[n * blk:(n + 1) * blk, :].astype(F32).T.astype(BF16)
    slope = slope_ref[0][:, :1]
    kmean = km_ref[0].astype(BF16)
    t_idx = lax.broadcasted_iota(jnp.int32, (blk, blk), 1)
    s_idx = lax.broadcasted_iota(jnp.int32, (blk, blk), 0)
    dist0 = (t_idx - s_idx).astype(F32)
    n_idx = lax.broadcasted_iota(jnp.int32, (nblk, blk), 0)

    def qblock(i, carry):
        qi = q_ref[pl.ds(pl.multiple_of(i * blk, blk), blk), :]
        gm = jnp.where(n_idx < i, _dot_nt(kmean, qi), NEG_BIG)
        rank = jnp.zeros((nblk, blk), F32)
        for mrow in range(nblk):
            gr = gm[mrow:mrow + 1, :]
            rank = rank + ((gr > gm) | ((gr == gm) & (mrow < n_idx))).astype(F32)
        sel_ref[...] = ((rank < MOBA_TOPK) & (n_idx < i)).astype(F32)

        ki = k_ref[pl.ds(pl.multiple_of(i * blk, blk), blk), :]
        s = _dot_nt(ki, qi) * scale - slope * dist0
        s = jnp.where(dist0 >= 0.0, s, NEG_BIG)
        m0 = jnp.max(s, axis=0, keepdims=True)
        p = jnp.exp(s - m0)
        l0 = jnp.sum(p, axis=0, keepdims=True)
        acc0 = _dot(vt_ref[i], p.astype(BF16))

        def kvblock(j, mla):
            m, l, acc = mla
            kj = k_ref[pl.ds(pl.multiple_of(j * blk, blk), blk), :]
            off = ((i - j) * blk).astype(F32)
            s = _dot_nt(kj, qi) * scale - slope * (dist0 + off)
            s = jnp.where(sel_ref[pl.ds(j, 1), :] > 0.0, s, NEG_BIG)
            m_new = jnp.maximum(m, jnp.max(s, axis=0, keepdims=True))
            alpha = jnp.exp(m - m_new)
            p = jnp.exp(s - m_new)
            l = alpha * l + jnp.sum(p, axis=0, keepdims=True)
            acc = alpha * acc + _dot(vt_ref[j], p.astype(BF16))
            return m_new, l, acc

        _, l, acc = lax.fori_loop(0, i, kvblock, (m0, l0, acc0))
        o_ref[pl.ds(pl.multiple_of(i * blk, blk), blk), :] = (acc / l).T.astype(BF16)
        return carry

    lax.fori_loop(0, nblk, qblock, 0)


def _moba(q, k, v, kmean, batch, seq):
    m, d = q.shape
    heads = d // HEAD_DIM
    nblk = seq // MOBA_BLOCK
    slopes = 2.0 ** (-8.0 * jnp.arange(1, heads + 1, dtype=F32) / heads)
    slopes = jnp.broadcast_to(slopes[:, None, None], (heads, 1, HEAD_DIM))
    spec = pl.BlockSpec((seq, HEAD_DIM), lambda b, h: (b, h))
    return pl.pallas_call(
        functools.partial(_moba_kernel, nblk),
        grid=(batch, heads),
        in_specs=[spec, spec, spec,
                  pl.BlockSpec((1, nblk, HEAD_DIM), lambda b, h: (b, 0, h)),
                  pl.BlockSpec((1, 1, HEAD_DIM), lambda b, h: (h, 0, 0))],
        out_specs=spec,
        out_shape=jax.ShapeDtypeStruct((m, d), BF16),
        scratch_shapes=[pltpu.VMEM((nblk, HEAD_DIM, MOBA_BLOCK), BF16),
                        pltpu.VMEM((nblk, MOBA_BLOCK), F32)],
        compiler_params=_params("arbitrary", "arbitrary"),
        name="moba_attn",
    )(q, k, v, kmean, slopes)


def _tile(n, pref):
    return pref if n % pref == 0 else n


def kernel(x, a_norm, a_w_in, a_head_norm, a_w_out, lower_bounds, kv_norm, w_kv, k_norm,
           b_norm, b_w_q, b_q_norm, b_w_o, mlp_norm, mlp_w1, mlp_w2):
    batch, seq, d = x.shape
    n_a = a_w_in.shape[0]
    n_b = b_w_q.shape[0]
    assert seq % MOBA_BLOCK == 0 and d % HEAD_DIM == 0
    m = batch * seq
    tm = _tile(m, 1024)
    ts = _tile(seq, 512)
    bf = lambda w: w.astype(BF16)

    h = x.reshape(m, d)
    kb = vb = kmean = None
    for l in range(n_a + n_b):
        if l < n_a:
            q, logf, k, v, gate = _hgrn_in(h, a_norm[l], lower_bounds, bf(a_w_in[l]), l, tm, 256)
            a = _gla(q, k, v, logf, gate, a_head_norm[l], batch, seq, ts)
            h = _proj_res(h, a, bf(a_w_out[l]), tm, 1024)
        else:
            if l == n_a:
                kb, vb, kmean = _kv(h, kv_norm, k_norm, bf(w_kv), tm, 512)
                kmean = kmean.reshape(batch, seq // MOBA_BLOCK, d)
            j = l - n_a
            q = _qproj(h, b_norm[j], b_q_norm[j], bf(b_w_q[j]), tm, 1024)
            a = _moba(q, kb, vb, kmean, batch, seq)
            h = _proj_res(h, a, bf(b_w_o[j]), tm, 1024)
        h = _mlp(h, mlp_norm[l], bf(mlp_w1[l]), bf(mlp_w2[l]), tm, 512)
    return h.reshape(batch, seq, d)
```

```python
import functools

import numpy as np
import jax
import jax.numpy as jnp
from jax import lax
from jax.experimental import pallas as pl
from jax.experimental.pallas import tpu as pltpu

F32 = jnp.float32
BF16 = jnp.bfloat16

HEAD_DIM = 128
MOBA_BLOCK = 256
MOBA_TOPK = 3
GLA_CHUNK = 64
EPS = 1e-6
NEG_BIG = -1e30
LB_FLOOR = 1e-30
LOG2E = 1.4426950408889634
SUBLANES = 8

V7X_VMEM_LIMIT_BYTES = 56 * 1024 * 1024


def _params(*sem):
    return pltpu.CompilerParams(dimension_semantics=sem, vmem_limit_bytes=V7X_VMEM_LIMIT_BYTES)


def _dot(a, b):
    return jnp.dot(a, b, preferred_element_type=F32)


def _dot_nt(a, b):
    return lax.dot_general(a, b, (((1,), (1,)), ((), ())), preferred_element_type=F32)


def _dot_tn(a, b):
    return lax.dot_general(a, b, (((0,), (0,)), ((), ())), preferred_element_type=F32)


def _rms(x, g):
    return x * lax.rsqrt(jnp.mean(x * x, axis=-1, keepdims=True) + EPS) * g


def _silu(x):
    return x * jax.nn.sigmoid(x)


def _log_sigmoid(x):
    return jnp.minimum(x, 0.0) - jnp.log1p(jnp.exp(-jnp.abs(x)))


def _logaddexp(a, b):
    return jnp.maximum(a, b) + jnp.log1p(jnp.exp(-jnp.abs(a - b)))


def _head_rms(x, g):
    outs = []
    for hh in range(x.shape[1] // HEAD_DIM):
        outs.append(_rms(x[:, hh * HEAD_DIM:(hh + 1) * HEAD_DIM], g))
    return outs


def _hgrn_in_kernel(layer, h_ref, ng_ref, lb_ref, wq_ref, wf_ref, wi_ref, wg_ref,
                    q_ref, lf_ref, k_ref, v_ref, gt_ref, xn_ref):
    @pl.when(pl.program_id(1) == 0)
    def _():
        xn_ref[...] = _rms(h_ref[...], ng_ref[...]).astype(BF16)

    xn = xn_ref[...]
    pq = _dot(xn, wq_ref[...])
    q_ref[...] = _silu(pq).astype(BF16)

    lbr = lb_ref[...]
    rows = [lbr[r:r + 1, :] for r in range(lbr.shape[0])]
    mx = functools.reduce(jnp.maximum, rows)
    ex = [jnp.exp(r - mx) for r in rows]
    den = functools.reduce(lambda a, b: a + b, ex)
    p = [e / den for e in ex]
    lb = functools.reduce(lambda a, b: a + b, p[:layer + 1]) - p[0]

    pf = _dot(xn, wf_ref[...])
    lf_ref[...] = _logaddexp(jnp.log(jnp.maximum(lb, LB_FLOOR)),
                             jnp.log1p(-lb) + _log_sigmoid(pf))
    k_ref[...] = ((1.0 - lb) * jax.nn.sigmoid(-pf)).astype(BF16)

    v_ref[...] = _dot(xn, wi_ref[...]).astype(BF16)
    gt_ref[...] = _silu(_dot(xn, wg_ref[...])).astype(BF16)


def _hgrn_in(h, norm_g, lower_bounds, w_in, layer, tm, tn):
    m, d = h.shape
    nj = d // tn
    wspec = lambda g: pl.BlockSpec((d, tn), lambda i, j, g=g: (0, j + g * nj))
    ospec = pl.BlockSpec((tm, tn), lambda i, j: (i, j))
    return pl.pallas_call(
        functools.partial(_hgrn_in_kernel, layer),
        grid=(m // tm, nj),
        in_specs=[pl.BlockSpec((tm, d), lambda i, j: (i, 0)),
                  pl.BlockSpec((1, d), lambda i, j: (0, 0)),
                  pl.BlockSpec((lower_bounds.shape[0], tn), lambda i, j: (0, j)),
                  wspec(0), wspec(1), wspec(2), wspec(3)],
        out_specs=[ospec] * 5,
        out_shape=[jax.ShapeDtypeStruct((m, d), BF16), jax.ShapeDtypeStruct((m, d), F32),
                   jax.ShapeDtypeStruct((m, d), BF16), jax.ShapeDtypeStruct((m, d), BF16),
                   jax.ShapeDtypeStruct((m, d), BF16)],
        scratch_shapes=[pltpu.VMEM((tm, d), BF16)],
        compiler_params=_params("arbitrary", "arbitrary"),
        name="hgrn_in",
    )(h, norm_g.reshape(1, d), lower_bounds, w_in, w_in, w_in, w_in)


def _gla_tables(c):
    r = np.arange(c)
    j = r[None, :]
    mats = [j <= r[:, None]]
    small = (4, 2, 1)
    for m in small:
        seg, pos = r // (2 * m), r % (2 * m)
        mid = (seg * 2 * m + m)[:, None]
        second = (pos >= m)[:, None]
        mats.append(np.where(second, (j >= mid) & (j <= r[:, None]), (j > r[:, None]) & (j <= mid - 1)))
    d = np.concatenate(mats, 0).astype(np.float32)
    dmat = np.concatenate([d, d, d], axis=1)
    lev = np.full((c, c), -1, np.int32)
    for li, m in enumerate(small):
        seg, pos = r // (2 * m), r % (2 * m)
        ok = (seg[:, None] == seg[None, :]) & (pos >= m)[:, None] & (pos < m)[None, :]
        lev[ok] = li
    big = []
    m = c // 2
    while m >= SUBLANES:
        t = np.concatenate([np.arange(s0 + m, s0 + 2 * m) for s0 in range(0, c, 2 * m)])
        ok = (t[:, None] // (2 * m) == r[None, :] // (2 * m)) & ((r % (2 * m)) < m)[None, :]
        big.append(ok.astype(np.float32))
        m //= 2
    return dmat, lev, np.stack(big)


def _gla_kernel(chunk, q_ref, k_ref, v_ref, g_ref, gt_ref, hg_ref, dmat_ref, lev_ref, big_ref,
                o_ref, st_ref):
    @pl.when(pl.program_id(2) == 0)
    def _():
        st_ref[...] = jnp.zeros_like(st_ref)

    c = chunk
    nc = q_ref.shape[0] // c
    lev = lev_ref[...]
    hg = hg_ref[...]

    g = g_ref[...]
    g_hi = g.astype(BF16)
    r1 = g - g_hi.astype(F32)
    g_mid = r1.astype(BF16)
    g_lo = (r1 - g_mid.astype(F32)).astype(BF16)
    lanes = lambda x: jnp.concatenate([x[i * c:(i + 1) * c] for i in range(nc)], axis=1)
    e_all = _dot(dmat_ref[...], jnp.concatenate([lanes(g_hi), lanes(g_mid), lanes(g_lo)], axis=0))

    small_p, big_p, big_tgt = [], [], []
    for ci in range(nc):
        sl = pl.ds(ci * c, c)
        cols = slice(ci * HEAD_DIM, (ci + 1) * HEAD_DIM)
        b = e_all[0:c, cols]
        q = q_ref[sl, :].astype(F32)
        k = k_ref[sl, :].astype(F32)
        ps = []
        for li in range(3):
            w = jnp.exp(e_all[(li + 1) * c:(li + 2) * c, cols])
            ps.append(_dot_nt((q * w).astype(BF16), (k * w).astype(BF16)))
        small_p.append(ps)
        ps, tg = [], []
        m = c // 2
        while m >= SUBLANES:
            qs, ks, tgt = [], [], []
            for s0 in range(0, c, 2 * m):
                ref = b[s0 + m - 1:s0 + m, :]
                qs.append(q[s0 + m:s0 + 2 * m] * jnp.exp(b[s0 + m:s0 + 2 * m] - ref))
                ks.append(k[s0:s0 + m] * jnp.exp(ref - b[s0:s0 + m]))
                ks.append(k[s0 + m:s0 + 2 * m])
                tgt.extend(range((s0 + m) // SUBLANES, (s0 + 2 * m) // SUBLANES))
            ps.append(_dot_nt(jnp.concatenate(qs, 0).astype(BF16), jnp.concatenate(ks, 0).astype(BF16)))
            tg.append(tgt)
            m //= 2
        big_p.append(ps)
        big_tgt.append(tg)

    intra, qbs, upds, dcols = [], [], [], []
    for ci in range(nc):
        sl = pl.ds(ci * c, c)
        cols = slice(ci * HEAD_DIM, (ci + 1) * HEAD_DIM)
        b = e_all[0:c, cols]
        q = q_ref[sl, :].astype(F32)
        k = k_ref[sl, :].astype(F32)
        v = v_ref[sl, :]
        rows = [jnp.zeros((SUBLANES, c), F32) for _ in range(c // SUBLANES)]
        for li, p in enumerate(small_p[ci]):
            p = jnp.where(lev == li, p, 0.0)
            rows = [rw + p[i * SUBLANES:(i + 1) * SUBLANES] for i, rw in enumerate(rows)]
        for li, (p, tgt) in enumerate(zip(big_p[ci], big_tgt[ci])):
            p = p * big_ref[li]
            for n, i in enumerate(tgt):
                rows[i] = rows[i] + p[n * SUBLANES:(n + 1) * SUBLANES]
        scores = jnp.concatenate(rows, 0).astype(BF16)
        wb = jnp.exp(b)
        we = jnp.exp(b[c - 1:c, :] - b)
        intra.append(_dot(scores, v) + jnp.sum(q * k, axis=-1, keepdims=True) * v.astype(F32))
        qbs.append((q * wb).astype(BF16))
        upds.append(_dot_tn((k * we).astype(BF16), v))
        dcols.append(jnp.broadcast_to(wb[c - 1:c, :], (SUBLANES, HEAD_DIM)).T[:, :1])

    st = st_ref[...]
    states = []
    for ci in range(nc):
        states.append(st.astype(BF16))
        st = st * dcols[ci] + upds[ci]
    st_ref[...] = st

    for ci in range(nc):
        sl = pl.ds(ci * c, c)
        o = _dot(qbs[ci], states[ci]) + intra[ci]
        o_ref[sl, :] = (_rms(o, hg) * gt_ref[sl, :].astype(F32)).astype(BF16)


def _gla(q, k, v, logf, gate, head_gain, batch, seq, ts):
    m, d = q.shape
    heads = d // HEAD_DIM
    ns = seq // ts
    dmat, lev, big = _gla_tables(GLA_CHUNK)
    spec = pl.BlockSpec((ts, HEAD_DIM), lambda b, h, s: (b * ns + s, h))
    const = lambda a: pl.BlockSpec(a.shape, lambda b, h, s: (0,) * a.ndim)
    return pl.pallas_call(
        functools.partial(_gla_kernel, GLA_CHUNK),
        grid=(batch, heads, ns),
        in_specs=[spec, spec, spec, spec, spec,
                  pl.BlockSpec((1, HEAD_DIM), lambda b, h, s: (0, h)),
                  const(dmat), const(lev), const(big)],
        out_specs=spec,
        out_shape=jax.ShapeDtypeStruct((m, d), BF16),
        scratch_shapes=[pltpu.VMEM((HEAD_DIM, HEAD_DIM), F32)],
        compiler_params=_params("arbitrary", "arbitrary", "arbitrary"),
        name="gla",
    )(q, k, v, logf, gate, head_gain.reshape(1, d), jnp.asarray(dmat, BF16), jnp.asarray(lev),
      jnp.asarray(big))


def _proj_res_kernel(h_ref, a_ref, w_ref, o_ref):
    o_ref[...] = h_ref[...] + _dot(a_ref[...], w_ref[...])


def _proj_res(h, a, w, tm, tn):
    m, d = h.shape
    kdim = a.shape[1]
    return pl.pallas_call(
        _proj_res_kernel,
        grid=(m // tm, d // tn),
        in_specs=[pl.BlockSpec((tm, tn), lambda i, j: (i, j)),
                  pl.BlockSpec((tm, kdim), lambda i, j: (i, 0)),
                  pl.BlockSpec((kdim, tn), lambda i, j: (0, j))],
        out_specs=pl.BlockSpec((tm, tn), lambda i, j: (i, j)),
        out_shape=jax.ShapeDtypeStruct((m, d), F32),
        compiler_params=_params("arbitrary", "arbitrary"),
        name="proj_res",
    )(h, a, w)


def _mlp_kernel(h_ref, ng_ref, w1_ref, w2_ref, o_ref, xn_ref):
    @pl.when(pl.program_id(1) == 0)
    def _():
        x = h_ref[...]
        xn_ref[...] = _rms(x, ng_ref[...]).astype(BF16)
        o_ref[...] = x

    t = jnp.square(jnp.maximum(_dot(xn_ref[...], w1_ref[...]), 0.0)).astype(BF16)
    o_ref[...] += _dot(t, w2_ref[...])


def _mlp(h, norm_g, w1, w2, tm, tf):
    m, d = h.shape
    ff = w1.shape[1]
    return pl.pallas_call(
        _mlp_kernel,
        grid=(m // tm, ff // tf),
        in_specs=[pl.BlockSpec((tm, d), lambda i, j: (i, 0)),
                  pl.BlockSpec((1, d), lambda i, j: (0, 0)),
                  pl.BlockSpec((d, tf), lambda i, j: (0, j)),
                  pl.BlockSpec((tf, d), lambda i, j: (j, 0))],
        out_specs=pl.BlockSpec((tm, d), lambda i, j: (i, 0)),
        out_shape=jax.ShapeDtypeStruct((m, d), F32),
        scratch_shapes=[pltpu.VMEM((tm, d), BF16)],
        compiler_params=_params("arbitrary", "arbitrary"),
        name="mlp",
    )(h, norm_g.reshape(1, d), w1, w2)


def _kv_kernel(h_ref, ng_ref, kn_ref, wk_ref, wv_ref, k_ref, v_ref, km_ref, xn_ref):
    @pl.when(pl.program_id(1) == 0)
    def _():
        xn_ref[...] = _rms(h_ref[...], ng_ref[...]).astype(BF16)

    xn = xn_ref[...]
    v_ref[...] = _dot(xn, wv_ref[...]).astype(BF16)
    heads = _head_rms(_dot(xn, wk_ref[...]), kn_ref[...])
    nblk = km_ref.shape[1]
    for hh, kh in enumerate(heads):
        cols = slice(hh * HEAD_DIM, (hh + 1) * HEAD_DIM)
        k_ref[:, cols] = kh.astype(BF16)
        km_ref[0, :, cols] = jnp.sum(kh.reshape(nblk, MOBA_BLOCK, HEAD_DIM), axis=1) * (1.0 / MOBA_BLOCK)


def _kv(h, norm_g, k_norm, w_kv, tm, tn):
    m, d = h.shape
    nj = d // tn
    nblk = tm // MOBA_BLOCK
    return pl.pallas_call(
        _kv_kernel,
        grid=(m // tm, nj),
        in_specs=[pl.BlockSpec((tm, d), lambda i, j: (i, 0)),
                  pl.BlockSpec((1, d), lambda i, j: (0, 0)),
                  pl.BlockSpec((1, HEAD_DIM), lambda i, j: (0, 0)),
                  pl.BlockSpec((d, tn), lambda i, j: (0, j)),
                  pl.BlockSpec((d, tn), lambda i, j: (0, j + nj))],
        out_specs=[pl.BlockSpec((tm, tn), lambda i, j: (i, j)),
                   pl.BlockSpec((tm, tn), lambda i, j: (i, j)),
                   pl.BlockSpec((1, nblk, tn), lambda i, j: (i, 0, j))],
        out_shape=[jax.ShapeDtypeStruct((m, d), BF16), jax.ShapeDtypeStruct((m, d), BF16),
                   jax.ShapeDtypeStruct((m // tm, nblk, d), F32)],
        scratch_shapes=[pltpu.VMEM((tm, d), BF16)],
        compiler_params=_params("arbitrary", "arbitrary"),
        name="shared_kv",
    )(h, norm_g.reshape(1, d), k_norm.reshape(1, HEAD_DIM), w_kv, w_kv)


def _qproj_kernel(h_ref, ng_ref, qn_ref, w_ref, q_ref, xn_ref):
    @pl.when(pl.program_id(1) == 0)
    def _():
        xn_ref[...] = _rms(h_ref[...], ng_ref[...]).astype(BF16)

    for hh, qh in enumerate(_head_rms(_dot(xn_ref[...], w_ref[...]), qn_ref[...])):
        q_ref[:, hh * HEAD_DIM:(hh + 1) * HEAD_DIM] = qh.astype(BF16)


def _qproj(h, norm_g, q_norm, w_q, tm, tn):
    m, d = h.shape
    return pl.pallas_call(
        _qproj_kernel,
        grid=(m // tm, d // tn),
        in_specs=[pl.BlockSpec((tm, d), lambda i, j: (i, 0)),
                  pl.BlockSpec((1, d), lambda i, j: (0, 0)),
                  pl.BlockSpec((1, HEAD_DIM), lambda i, j: (0, 0)),
                  pl.BlockSpec((d, tn), lambda i, j: (0, j))],
        out_specs=pl.BlockSpec((tm, tn), lambda i, j: (i, j)),
        out_shape=jax.ShapeDtypeStruct((m, d), BF16),
        scratch_shapes=[pltpu.VMEM((tm, d), BF16)],
        compiler_params=_params("arbitrary", "arbitrary"),
        name="moba_q",
    )(h, norm_g.reshape(1, d), q_norm.reshape(1, HEAD_DIM), w_q)


def _moba_kernel(nblk, q_ref, k_ref, v_ref, km_ref, slope_ref, o_ref, vt_ref, bias_ref):
    blk = MOBA_BLOCK
    grp = blk // SUBLANES
    scale = LOG2E * HEAD_DIM ** -0.5
    for n in range(nblk):
        vt_ref[:, n * blk:(n + 1) * blk] = v_ref[n * blk:(n + 1) * blk, :].astype(F32).T.astype(BF16)
    slope = slope_ref[0][:, :1]
    kmean = km_ref[0].astype(BF16)
    t_idx = lax.broadcasted_iota(jnp.int32, (blk, blk), 1)
    s_idx = lax.broadcasted_iota(jnp.int32, (blk, blk), 0)
    dist0 = (t_idx - s_idx).astype(F32)
    bias_ref[0] = jnp.where(dist0 >= 0.0, (LOG2E * slope) * dist0, -NEG_BIG)
    for dlt in range(1, nblk):
        bias_ref[dlt] = (LOG2E * slope) * (dist0 + float(dlt * blk))
    n_idx = lax.broadcasted_iota(jnp.int32, (nblk, blk), 0)

    for i in range(nblk):
        qi = q_ref[i * blk:(i + 1) * blk, :]
        sel = None
        if i > MOBA_TOPK:
            gm = jnp.where(n_idx < i, _dot_nt(kmean, qi), NEG_BIG)
            rank = jnp.zeros((nblk, blk), F32)
            for mrow in range(nblk):
                gr = gm[mrow:mrow + 1, :]
                rank = rank + ((gr > gm) | ((gr == gm) & (mrow < n_idx))).astype(F32)
            sel = ((rank < MOBA_TOPK) & (n_idx < i)).astype(F32)
        s_all = _dot_nt(k_ref[0:(i + 1) * blk, :], qi)
        s = []
        for j in range(i + 1):
            sj = s_all[j * blk:(j + 1) * blk].reshape(grp, SUBLANES, blk) * scale \
                - bias_ref[i - j].reshape(grp, SUBLANES, blk)
            if sel is not None and j < i:
                keep = jnp.broadcast_to(sel[j:j + 1, :], (SUBLANES, blk)) > 0.0
                sj = jnp.where(keep[None], sj, NEG_BIG)
            s.append(sj)
        m8 = functools.reduce(jnp.maximum, [jnp.max(sj, axis=0) for sj in s])
        m = jnp.broadcast_to(jnp.max(m8, axis=0, keepdims=True), (SUBLANES, blk))
        p = [jnp.exp2(sj - m[None]) for sj in s]
        l8 = functools.reduce(lambda a, b: a + b, [jnp.sum(pj, axis=0) for pj in p])
        l = jnp.sum(l8, axis=0, keepdims=True)
        pcat = jnp.concatenate([pj.reshape(blk, blk).astype(BF16) for pj in p], axis=0)
        acc = _dot(vt_ref[:, 0:(i + 1) * blk], pcat)
        o_ref[i * blk:(i + 1) * blk, :] = (acc / l).T.astype(BF16)


def _moba(q, k, v, kmean, batch, seq):
    m, d = q.shape
    heads = d // HEAD_DIM
    nblk = seq // MOBA_BLOCK
    slopes = 2.0 ** (-8.0 * jnp.arange(1, heads + 1, dtype=F32) / heads)
    slopes = jnp.broadcast_to(slopes[:, None, None], (heads, 1, HEAD_DIM))
    spec = pl.BlockSpec((seq, HEAD_DIM), lambda b, h: (b, h))
    return pl.pallas_call(
        functools.partial(_moba_kernel, nblk),
        grid=(batch, heads),
        in_specs=[spec, spec, spec,
                  pl.BlockSpec((1, nblk, HEAD_DIM), lambda b, h: (b, 0, h)),
                  pl.BlockSpec((1, 1, HEAD_DIM), lambda b, h: (h, 0, 0))],
        out_specs=spec,
        out_shape=jax.ShapeDtypeStruct((m, d), BF16),
        scratch_shapes=[pltpu.VMEM((HEAD_DIM, seq), BF16),
                        pltpu.VMEM((nblk, MOBA_BLOCK, MOBA_BLOCK), F32)],
        compiler_params=_params("arbitrary", "arbitrary"),
        name="moba_attn",
    )(q, k, v, kmean, slopes)


def _tile(n, pref):
    return pref if n % pref == 0 else n


def kernel(x, a_norm, a_w_in, a_head_norm, a_w_out, lower_bounds, kv_norm, w_kv, k_norm,
           b_norm, b_w_q, b_q_norm, b_w_o, mlp_norm, mlp_w1, mlp_w2):
    batch, seq, d = x.shape
    n_a = a_w_in.shape[0]
    n_b = b_w_q.shape[0]
    assert seq % MOBA_BLOCK == 0 and d % HEAD_DIM == 0
    m = batch * seq
    tm = _tile(m, 1024)
    ts = _tile(seq, 512)
    bf = lambda w: w.astype(BF16)

    h = x.reshape(m, d)
    kb = vb = kmean = None
    for l in range(n_a + n_b):
        if l < n_a:
            q, logf, k, v, gate = _hgrn_in(h, a_norm[l], lower_bounds, bf(a_w_in[l]), l, tm, 256)
            a = _gla(q, k, v, logf, gate, a_head_norm[l], batch, seq, ts)
            h = _proj_res(h, a, bf(a_w_out[l]), tm, 1024)
        else:
            if l == n_a:
                kb, vb, kmean = _kv(h, kv_norm, k_norm, bf(w_kv), tm, 512)
                kmean = kmean.reshape(batch, seq // MOBA_BLOCK, d)
            j = l - n_a
            q = _qproj(h, b_norm[j], b_q_norm[j], bf(b_w_q[j]), tm, 1024)
            a = _moba(q, kb, vb, kmean, batch, seq)
            h = _proj_res(h, a, bf(b_w_o[j]), tm, 1024)
        h = _mlp(h, mlp_norm[l], bf(mlp_w1[l]), bf(mlp_w2[l]), tm, 512)
    return h.reshape(batch, seq, d)
```

```python
import functools

import numpy as np
import jax
import jax.numpy as jnp
from jax import lax
from jax.experimental import pallas as pl
from jax.experimental.pallas import tpu as pltpu

F32 = jnp.float32
BF16 = jnp.bfloat16

HEAD_DIM = 128
MOBA_BLOCK = 256
MOBA_TOPK = 3
GLA_CHUNK = 64
EPS = 1e-6
NEG_BIG = -1e30
LB_FLOOR = 1e-30
LOG2E = 1.4426950408889634
SUBLANES = 8

V7X_VMEM_LIMIT_BYTES = 56 * 1024 * 1024


def _params(*sem):
    return pltpu.CompilerParams(dimension_semantics=sem, vmem_limit_bytes=V7X_VMEM_LIMIT_BYTES)


def _dot(a, b):
    return jnp.dot(a, b, preferred_element_type=F32)


def _wdot(a, w_ref):
    return jnp.dot(a, w_ref[...].astype(BF16), preferred_element_type=F32)


def _dot_nt(a, b):
    return lax.dot_general(a, b, (((1,), (1,)), ((), ())), preferred_element_type=F32)


def _dot_tn(a, b):
    return lax.dot_general(a, b, (((0,), (0,)), ((), ())), preferred_element_type=F32)


def _layer_spec(layer, block, index):
    return pl.BlockSpec((None,) + block, lambda i, j: (layer,) + index(i, j))


def _rms(x, g):
    return x * lax.rsqrt(jnp.mean(x * x, axis=-1, keepdims=True) + EPS) * g


def _silu(x):
    return x * jax.nn.sigmoid(x)


def _log_sigmoid(x):
    return jnp.minimum(x, 0.0) - jnp.log1p(jnp.exp(-jnp.abs(x)))


def _logaddexp(a, b):
    return jnp.maximum(a, b) + jnp.log1p(jnp.exp(-jnp.abs(a - b)))


def _head_rms(x, g):
    outs = []
    for hh in range(x.shape[1] // HEAD_DIM):
        outs.append(_rms(x[:, hh * HEAD_DIM:(hh + 1) * HEAD_DIM], g))
    return outs


def _hgrn_in_kernel(layer, h_ref, ng_ref, lb_ref, wq_ref, wf_ref, wi_ref, wg_ref,
                    q_ref, lf_ref, k_ref, v_ref, gt_ref, xn_ref):
    @pl.when(pl.program_id(1) == 0)
    def _():
        xn_ref[...] = _rms(h_ref[...], ng_ref[...]).astype(BF16)

    xn = xn_ref[...]
    pq = _wdot(xn, wq_ref)
    q_ref[...] = _silu(pq).astype(BF16)

    lbr = lb_ref[...]
    rows = [lbr[r:r + 1, :] for r in range(lbr.shape[0])]
    mx = functools.reduce(jnp.maximum, rows)
    ex = [jnp.exp(r - mx) for r in rows]
    den = functools.reduce(lambda a, b: a + b, ex)
    p = [e / den for e in ex]
    lb = functools.reduce(lambda a, b: a + b, p[:layer + 1]) - p[0]

    pf = _wdot(xn, wf_ref)
    lf_ref[...] = _logaddexp(jnp.log(jnp.maximum(lb, LB_FLOOR)),
                             jnp.log1p(-lb) + _log_sigmoid(pf))
    k_ref[...] = ((1.0 - lb) * jax.nn.sigmoid(-pf)).astype(BF16)

    v_ref[...] = _wdot(xn, wi_ref).astype(BF16)
    gt_ref[...] = _silu(_wdot(xn, wg_ref)).astype(BF16)


def _hgrn_in(h, norm_g, lower_bounds, w_in, layer, tm, tn):
    m, d = h.shape
    nj = d // tn
    wspec = lambda g: _layer_spec(layer, (d, tn), lambda i, j: (0, j + g * nj))
    ospec = pl.BlockSpec((tm, tn), lambda i, j: (i, j))
    return pl.pallas_call(
        functools.partial(_hgrn_in_kernel, layer),
        grid=(m // tm, nj),
        in_specs=[pl.BlockSpec((tm, d), lambda i, j: (i, 0)),
                  pl.BlockSpec((1, d), lambda i, j: (0, 0)),
                  pl.BlockSpec((lower_bounds.shape[0], tn), lambda i, j: (0, j)),
                  wspec(0), wspec(1), wspec(2), wspec(3)],
        out_specs=[ospec] * 5,
        out_shape=[jax.ShapeDtypeStruct((m, d), BF16), jax.ShapeDtypeStruct((m, d), F32),
                   jax.ShapeDtypeStruct((m, d), BF16), jax.ShapeDtypeStruct((m, d), BF16),
                   jax.ShapeDtypeStruct((m, d), BF16)],
        scratch_shapes=[pltpu.VMEM((tm, d), BF16)],
        compiler_params=_params("arbitrary", "arbitrary"),
        name="hgrn_in",
    )(h, norm_g.reshape(1, d), lower_bounds, w_in, w_in, w_in, w_in)


def _gla_tables(c):
    r = np.arange(c)
    j = r[None, :]
    mats = [j <= r[:, None]]
    small = (4, 2, 1)
    for m in small:
        seg, pos = r // (2 * m), r % (2 * m)
        mid = (seg * 2 * m + m)[:, None]
        second = (pos >= m)[:, None]
        mats.append(np.where(second, (j >= mid) & (j <= r[:, None]), (j > r[:, None]) & (j <= mid - 1)))
    d = np.concatenate(mats, 0).astype(np.float32)
    dmat = np.concatenate([d, d, d], axis=1)
    lev = np.full((c, c), -1, np.int32)
    for li, m in enumerate(small):
        seg, pos = r // (2 * m), r % (2 * m)
        ok = (seg[:, None] == seg[None, :]) & (pos >= m)[:, None] & (pos < m)[None, :]
        lev[ok] = li
    big = []
    m = c // 2
    while m >= SUBLANES:
        t = np.concatenate([np.arange(s0 + m, s0 + 2 * m) for s0 in range(0, c, 2 * m)])
        ok = (t[:, None] // (2 * m) == r[None, :] // (2 * m)) & ((r % (2 * m)) < m)[None, :]
        big.append(ok.astype(np.float32))
        m //= 2
    return dmat, lev, np.stack(big)


def _gla_kernel(chunk, q_ref, k_ref, v_ref, g_ref, gt_ref, hg_ref, dmat_ref, lev_ref, big_ref,
                o_ref, st_ref):
    @pl.when(pl.program_id(2) == 0)
    def _():
        st_ref[...] = jnp.zeros_like(st_ref)

    c = chunk
    nc = q_ref.shape[0] // c
    lev = lev_ref[...]
    hg = hg_ref[...]

    g = g_ref[...]
    g_hi = g.astype(BF16)
    r1 = g - g_hi.astype(F32)
    g_mid = r1.astype(BF16)
    g_lo = (r1 - g_mid.astype(F32)).astype(BF16)
    lanes = lambda x: jnp.concatenate([x[i * c:(i + 1) * c] for i in range(nc)], axis=1)
    e_all = _dot(dmat_ref[...], jnp.concatenate([lanes(g_hi), lanes(g_mid), lanes(g_lo)], axis=0))

    small_p, big_p, big_tgt = [], [], []
    for ci in range(nc):
        sl = pl.ds(ci * c, c)
        cols = slice(ci * HEAD_DIM, (ci + 1) * HEAD_DIM)
        b = e_all[0:c, cols]
        q = q_ref[sl, :].astype(F32)
        k = k_ref[sl, :].astype(F32)
        ps = []
        for li in range(3):
            w = jnp.exp(e_all[(li + 1) * c:(li + 2) * c, cols])
            ps.append(_dot_nt((q * w).astype(BF16), (k * w).astype(BF16)))
        small_p.append(ps)
        ps, tg = [], []
        m = c // 2
        while m >= SUBLANES:
            qs, ks, tgt = [], [], []
            for s0 in range(0, c, 2 * m):
                ref = b[s0 + m - 1:s0 + m, :]
                qs.append(q[s0 + m:s0 + 2 * m] * jnp.exp(b[s0 + m:s0 + 2 * m] - ref))
                ks.append(k[s0:s0 + m] * jnp.exp(ref - b[s0:s0 + m]))
                ks.append(k[s0 + m:s0 + 2 * m])
                tgt.extend(range((s0 + m) // SUBLANES, (s0 + 2 * m) // SUBLANES))
            ps.append(_dot_nt(jnp.concatenate(qs, 0).astype(BF16), jnp.concatenate(ks, 0).astype(BF16)))
            tg.append(tgt)
            m //= 2
        big_p.append(ps)
        big_tgt.append(tg)

    intra, qbs, upds, dcols = [], [], [], []
    for ci in range(nc):
        sl = pl.ds(ci * c, c)
        cols = slice(ci * HEAD_DIM, (ci + 1) * HEAD_DIM)
        b = e_all[0:c, cols]
        q = q_ref[sl, :].astype(F32)
        k = k_ref[sl, :].astype(F32)
        v = v_ref[sl, :]
        rows = [jnp.zeros((SUBLANES, c), F32) for _ in range(c // SUBLANES)]
        for li, p in enumerate(small_p[ci]):
            p = jnp.where(lev == li, p, 0.0)
            rows = [rw + p[i * SUBLANES:(i + 1) * SUBLANES] for i, rw in enumerate(rows)]
        for li, (p, tgt) in enumerate(zip(big_p[ci], big_tgt[ci])):
            p = p * big_ref[li]
            for n, i in enumerate(tgt):
                rows[i] = rows[i] + p[n * SUBLANES:(n + 1) * SUBLANES]
        scores = jnp.concatenate(rows, 0).astype(BF16)
        wb = jnp.exp(b)
        we = jnp.exp(b[c - 1:c, :] - b)
        intra.append(_dot(scores, v) + jnp.sum(q * k, axis=-1, keepdims=True) * v.astype(F32))
        qbs.append((q * wb).astype(BF16))
        upds.append(_dot_tn((k * we).astype(BF16), v))
        dcols.append(jnp.broadcast_to(wb[c - 1:c, :], (SUBLANES, HEAD_DIM)).T[:, :1])

    st = st_ref[...]
    states = []
    for ci in range(nc):
        states.append(st.astype(BF16))
        st = st * dcols[ci] + upds[ci]
    st_ref[...] = st

    for ci in range(nc):
        sl = pl.ds(ci * c, c)
        o = _dot(qbs[ci], states[ci]) + intra[ci]
        o_ref[sl, :] = (_rms(o, hg) * gt_ref[sl, :].astype(F32)).astype(BF16)


def _gla(q, k, v, logf, gate, head_gain, batch, seq, ts):
    m, d = q.shape
    heads = d // HEAD_DIM
    ns = seq // ts
    dmat, lev, big = _gla_tables(GLA_CHUNK)
    spec = pl.BlockSpec((ts, HEAD_DIM), lambda b, h, s: (b * ns + s, h))
    const = lambda a: pl.BlockSpec(a.shape, lambda b, h, s: (0,) * a.ndim)
    return pl.pallas_call(
        functools.partial(_gla_kernel, GLA_CHUNK),
        grid=(batch, heads, ns),
        in_specs=[spec, spec, spec, spec, spec,
                  pl.BlockSpec((1, HEAD_DIM), lambda b, h, s: (0, h)),
                  const(dmat), const(lev), const(big)],
        out_specs=spec,
        out_shape=jax.ShapeDtypeStruct((m, d), BF16),
        scratch_shapes=[pltpu.VMEM((HEAD_DIM, HEAD_DIM), F32)],
        compiler_params=_params("arbitrary", "arbitrary", "arbitrary"),
        name="gla",
    )(q, k, v, logf, gate, head_gain.reshape(1, d), jnp.asarray(dmat, BF16), jnp.asarray(lev),
      jnp.asarray(big))


def _proj_res_kernel(h_ref, a_ref, w_ref, o_ref):
    o_ref[...] = h_ref[...] + _wdot(a_ref[...], w_ref)


def _proj_res(h, a, w, layer, tm, tn):
    m, d = h.shape
    kdim = a.shape[1]
    return pl.pallas_call(
        _proj_res_kernel,
        grid=(m // tm, d // tn),
        in_specs=[pl.BlockSpec((tm, tn), lambda i, j: (i, j)),
                  pl.BlockSpec((tm, kdim), lambda i, j: (i, 0)),
                  _layer_spec(layer, (kdim, tn), lambda i, j: (0, j))],
        out_specs=pl.BlockSpec((tm, tn), lambda i, j: (i, j)),
        out_shape=jax.ShapeDtypeStruct((m, d), F32),
        compiler_params=_params("arbitrary", "arbitrary"),
        name="proj_res",
    )(h, a, w)


def _mlp_kernel(h_ref, ng_ref, w1_ref, w2_ref, o_ref, xn_ref):
    @pl.when(pl.program_id(1) == 0)
    def _():
        x = h_ref[...]
        xn_ref[...] = _rms(x, ng_ref[...]).astype(BF16)
        o_ref[...] = x

    t = jnp.square(jnp.maximum(_wdot(xn_ref[...], w1_ref), 0.0)).astype(BF16)
    o_ref[...] += _wdot(t, w2_ref)


def _mlp(h, norm_g, w1, w2, layer, tm, tf):
    m, d = h.shape
    ff = w1.shape[-1]
    return pl.pallas_call(
        _mlp_kernel,
        grid=(m // tm, ff // tf),
        in_specs=[pl.BlockSpec((tm, d), lambda i, j: (i, 0), pipeline_mode=pl.Buffered(1)),
                  pl.BlockSpec((1, d), lambda i, j: (0, 0)),
                  _layer_spec(layer, (d, tf), lambda i, j: (0, j)),
                  _layer_spec(layer, (tf, d), lambda i, j: (j, 0))],
        out_specs=pl.BlockSpec((tm, d), lambda i, j: (i, 0)),
        out_shape=jax.ShapeDtypeStruct((m, d), F32),
        scratch_shapes=[pltpu.VMEM((tm, d), BF16)],
        compiler_params=_params("arbitrary", "arbitrary"),
        name="mlp",
    )(h, norm_g.reshape(1, d), w1, w2)


def _kv_kernel(h_ref, ng_ref, kn_ref, wk_ref, wv_ref, k_ref, v_ref, km_ref, xn_ref):
    @pl.when(pl.program_id(1) == 0)
    def _():
        xn_ref[...] = _rms(h_ref[...], ng_ref[...]).astype(BF16)

    xn = xn_ref[...]
    v_ref[...] = _wdot(xn, wv_ref).astype(BF16)
    heads = _head_rms(_wdot(xn, wk_ref), kn_ref[...])
    nblk = km_ref.shape[1]
    for hh, kh in enumerate(heads):
        cols = slice(hh * HEAD_DIM, (hh + 1) * HEAD_DIM)
        k_ref[:, cols] = kh.astype(BF16)
        km_ref[0, :, cols] = jnp.sum(kh.reshape(nblk, MOBA_BLOCK, HEAD_DIM), axis=1) * (1.0 / MOBA_BLOCK)


def _kv(h, norm_g, k_norm, w_kv, tm, tn):
    m, d = h.shape
    nj = d // tn
    nblk = tm // MOBA_BLOCK
    return pl.pallas_call(
        _kv_kernel,
        grid=(m // tm, nj),
        in_specs=[pl.BlockSpec((tm, d), lambda i, j: (i, 0)),
                  pl.BlockSpec((1, d), lambda i, j: (0, 0)),
                  pl.BlockSpec((1, HEAD_DIM), lambda i, j: (0, 0)),
                  pl.BlockSpec((d, tn), lambda i, j: (0, j)),
                  pl.BlockSpec((d, tn), lambda i, j: (0, j + nj))],
        out_specs=[pl.BlockSpec((tm, tn), lambda i, j: (i, j)),
                   pl.BlockSpec((tm, tn), lambda i, j: (i, j)),
                   pl.BlockSpec((1, nblk, tn), lambda i, j: (i, 0, j))],
        out_shape=[jax.ShapeDtypeStruct((m, d), BF16), jax.ShapeDtypeStruct((m, d), BF16),
                   jax.ShapeDtypeStruct((m // tm, nblk, d), F32)],
        scratch_shapes=[pltpu.VMEM((tm, d), BF16)],
        compiler_params=_params("arbitrary", "arbitrary"),
        name="shared_kv",
    )(h, norm_g.reshape(1, d), k_norm.reshape(1, HEAD_DIM), w_kv, w_kv)


def _qproj_kernel(h_ref, ng_ref, qn_ref, w_ref, q_ref, xn_ref):
    @pl.when(pl.program_id(1) == 0)
    def _():
        xn_ref[...] = _rms(h_ref[...], ng_ref[...]).astype(BF16)

    for hh, qh in enumerate(_head_rms(_wdot(xn_ref[...], w_ref), qn_ref[...])):
        q_ref[:, hh * HEAD_DIM:(hh + 1) * HEAD_DIM] = qh.astype(BF16)


def _qproj(h, norm_g, q_norm, w_q, layer, tm, tn):
    m, d = h.shape
    return pl.pallas_call(
        _qproj_kernel,
        grid=(m // tm, d // tn),
        in_specs=[pl.BlockSpec((tm, d), lambda i, j: (i, 0)),
                  pl.BlockSpec((1, d), lambda i, j: (0, 0)),
                  pl.BlockSpec((1, HEAD_DIM), lambda i, j: (0, 0)),
                  _layer_spec(layer, (d, tn), lambda i, j: (0, j))],
        out_specs=pl.BlockSpec((tm, tn), lambda i, j: (i, j)),
        out_shape=jax.ShapeDtypeStruct((m, d), BF16),
        scratch_shapes=[pltpu.VMEM((tm, d), BF16)],
        compiler_params=_params("arbitrary", "arbitrary"),
        name="moba_q",
    )(h, norm_g.reshape(1, d), q_norm.reshape(1, HEAD_DIM), w_q)


def _moba_kernel(nblk, q_ref, k_ref, v_ref, km_ref, slope_ref, o_ref, vt_ref, bias_ref):
    blk = MOBA_BLOCK
    grp = blk // SUBLANES
    scale = LOG2E * HEAD_DIM ** -0.5
    for n in range(nblk):
        vt_ref[:, n * blk:(n + 1) * blk] = v_ref[n * blk:(n + 1) * blk, :].astype(F32).T.astype(BF16)
    slope = slope_ref[0][:, :1]
    kmean = km_ref[0].astype(BF16)
    t_idx = lax.broadcasted_iota(jnp.int32, (blk, blk), 1)
    s_idx = lax.broadcasted_iota(jnp.int32, (blk, blk), 0)
    dist0 = (t_idx - s_idx).astype(F32)
    bias_ref[0] = jnp.where(dist0 >= 0.0, (LOG2E * slope) * dist0, -NEG_BIG)
    for dlt in range(1, nblk):
        bias_ref[dlt] = (LOG2E * slope) * (dist0 + float(dlt * blk))
    n_idx = lax.broadcasted_iota(jnp.int32, (nblk, blk), 0)

    for i in range(nblk):
        qi = q_ref[i * blk:(i + 1) * blk, :]
        sel = None
        if i > MOBA_TOPK:
            gm = jnp.where(n_idx < i, _dot_nt(kmean, qi), NEG_BIG)
            rank = jnp.zeros((nblk, blk), F32)
            for mrow in range(nblk):
                gr = gm[mrow:mrow + 1, :]
                rank = rank + ((gr > gm) | ((gr == gm) & (mrow < n_idx))).astype(F32)
            sel = ((rank < MOBA_TOPK) & (n_idx < i)).astype(F32)
        s_all = _dot_nt(k_ref[0:(i + 1) * blk, :], qi)
        s = []
        for j in range(i + 1):
            sj = s_all[j * blk:(j + 1) * blk].reshape(grp, SUBLANES, blk) * scale \
                - bias_ref[i - j].reshape(grp, SUBLANES, blk)
            if sel is not None and j < i:
                keep = jnp.broadcast_to(sel[j:j + 1, :], (SUBLANES, blk)) > 0.0
                sj = jnp.where(keep[None], sj, NEG_BIG)
            s.append(sj)
        m8 = functools.reduce(jnp.maximum, [jnp.max(sj, axis=0) for sj in s])
        m = jnp.broadcast_to(jnp.max(m8, axis=0, keepdims=True), (SUBLANES, blk))
        p = [jnp.exp2(sj - m[None]) for sj in s]
        l8 = functools.reduce(lambda a, b: a + b, [jnp.sum(pj, axis=0) for pj in p])
        l = jnp.sum(l8, axis=0, keepdims=True)
        pcat = jnp.concatenate([pj.reshape(blk, blk).astype(BF16) for pj in p], axis=0)
        acc = _dot(vt_ref[:, 0:(i + 1) * blk], pcat)
        o_ref[i * blk:(i + 1) * blk, :] = (acc / l).T.astype(BF16)


def _moba(q, k, v, kmean, batch, seq):
    m, d = q.shape
    heads = d // HEAD_DIM
    nblk = seq // MOBA_BLOCK
    slopes = 2.0 ** (-8.0 * jnp.arange(1, heads + 1, dtype=F32) / heads)
    slopes = jnp.broadcast_to(slopes[:, None, None], (heads, 1, HEAD_DIM))
    spec = pl.BlockSpec((seq, HEAD_DIM), lambda b, h: (b, h))
    return pl.pallas_call(
        functools.partial(_moba_kernel, nblk),
        grid=(batch, heads),
        in_specs=[spec, spec, spec,
                  pl.BlockSpec((1, nblk, HEAD_DIM), lambda b, h: (b, 0, h)),
                  pl.BlockSpec((1, 1, HEAD_DIM), lambda b, h: (h, 0, 0))],
        out_specs=spec,
        out_shape=jax.ShapeDtypeStruct((m, d), BF16),
        scratch_shapes=[pltpu.VMEM((HEAD_DIM, seq), BF16),
                        pltpu.VMEM((nblk, MOBA_BLOCK, MOBA_BLOCK), F32)],
        compiler_params=_params("arbitrary", "arbitrary"),
        name="moba_attn",
    )(q, k, v, kmean, slopes)


def _tile(n, pref):
    return pref if n % pref == 0 else n


def kernel(x, a_norm, a_w_in, a_head_norm, a_w_out, lower_bounds, kv_norm, w_kv, k_norm,
           b_norm, b_w_q, b_q_norm, b_w_o, mlp_norm, mlp_w1, mlp_w2):
    batch, seq, d = x.shape
    n_a = a_w_in.shape[0]
    n_b = b_w_q.shape[0]
    assert seq % MOBA_BLOCK == 0 and d % HEAD_DIM == 0
    m = batch * seq
    tm = _tile(m, 1024)
    ts = _tile(seq, 512)

    h = x.reshape(m, d)
    kb = vb = kmean = None
    for l in range(n_a + n_b):
        if l < n_a:
            q, logf, k, v, gate = _hgrn_in(h, a_norm[l], lower_bounds, a_w_in, l, tm, 256)
            a = _gla(q, k, v, logf, gate, a_head_norm[l], batch, seq, ts)
            h = _proj_res(h, a, a_w_out, l, tm, 1024)
        else:
            if l == n_a:
                kb, vb, kmean = _kv(h, kv_norm, k_norm, w_kv, tm, 512)
                kmean = kmean.reshape(batch, seq // MOBA_BLOCK, d)
            j = l - n_a
            q = _qproj(h, b_norm[j], b_q_norm[j], b_w_q, j, tm, 1024)
            a = _moba(q, kb, vb, kmean, batch, seq)
            h = _proj_res(h, a, b_w_o, j, tm, 1024)
        h = _mlp(h, mlp_norm[l], mlp_w1, mlp_w2, l, tm, 512)
    return h.reshape(batch, seq, d)
```

```python
import functools

import numpy as np
import jax
import jax.numpy as jnp
from jax import lax
from jax.experimental import pallas as pl
from jax.experimental.pallas import tpu as pltpu

F32 = jnp.float32
BF16 = jnp.bfloat16

HEAD_DIM = 128
MOBA_BLOCK = 256
MOBA_TOPK = 3
GLA_CHUNK = 64
EPS = 1e-6
NEG_BIG = -1e30
LB_FLOOR = 1e-30
LOG2E = 1.4426950408889634
SUBLANES = 8

V7X_VMEM_LIMIT_BYTES = 56 * 1024 * 1024


def _params(*sem):
    return pltpu.CompilerParams(dimension_semantics=sem, vmem_limit_bytes=V7X_VMEM_LIMIT_BYTES)


def _dot(a, b):
    return jnp.dot(a, b, preferred_element_type=F32)


def _wdot(a, w_ref):
    return jnp.dot(a, w_ref[...].astype(BF16), preferred_element_type=F32)


def _dot_nt(a, b):
    return lax.dot_general(a, b, (((1,), (1,)), ((), ())), preferred_element_type=F32)


def _dot_tn(a, b):
    return lax.dot_general(a, b, (((0,), (0,)), ((), ())), preferred_element_type=F32)


def _layer_spec(layer, block, index):
    return pl.BlockSpec((None,) + block, lambda i, j: (layer,) + index(i, j))


def _rms(x, g):
    return x * lax.rsqrt(jnp.mean(x * x, axis=-1, keepdims=True) + EPS) * g


def _silu(x):
    return x * jax.nn.sigmoid(x)


def _head_rms(x, g):
    outs = []
    for hh in range(x.shape[1] // HEAD_DIM):
        outs.append(_rms(x[:, hh * HEAD_DIM:(hh + 1) * HEAD_DIM], g))
    return outs


def _hgrn_in_kernel(layer, h_ref, ng_ref, lb_ref, wq_ref, wf_ref, wi_ref, wg_ref,
                    q_ref, lf_ref, k_ref, v_ref, gt_ref, xn_ref):
    @pl.when(pl.program_id(1) == 0)
    def _():
        xn_ref[...] = _rms(h_ref[...], ng_ref[...]).astype(BF16)

    xn = xn_ref[...]
    pf = _wdot(xn, wf_ref)
    pq = _wdot(xn, wq_ref)
    pg = _wdot(xn, wg_ref)
    pv = _wdot(xn, wi_ref)

    lbr = lb_ref[...]
    rows = [lbr[r:r + 1, :] for r in range(lbr.shape[0])]
    mx = functools.reduce(jnp.maximum, rows)
    ex = [jnp.exp(r - mx) for r in rows]
    den = functools.reduce(lambda a, b: a + b, ex)
    p = [e / den for e in ex]
    lb = functools.reduce(lambda a, b: a + b, p[:layer + 1]) - p[0]

    e = jnp.exp(-jnp.abs(pf))
    r = 1.0 / (1.0 + e)
    er = e * r
    pos = pf >= 0.0
    one_m = 1.0 - lb
    lf_ref[...] = jnp.log(jnp.maximum(lb, LB_FLOOR) + one_m * jnp.where(pos, r, er))
    k_ref[...] = (one_m * jnp.where(pos, er, r)).astype(BF16)
    q_ref[...] = _silu(pq).astype(BF16)
    gt_ref[...] = _silu(pg).astype(BF16)
    v_ref[...] = pv.astype(BF16)


def _hgrn_in(h, norm_g, lower_bounds, w_in, layer, tm, tn):
    m, d = h.shape
    nj = d // tn
    wspec = lambda g: _layer_spec(layer, (d, tn), lambda i, j: (0, j + g * nj))
    ospec = pl.BlockSpec((tm, tn), lambda i, j: (i, j))
    return pl.pallas_call(
        functools.partial(_hgrn_in_kernel, layer),
        grid=(m // tm, nj),
        in_specs=[pl.BlockSpec((tm, d), lambda i, j: (i, 0)),
                  pl.BlockSpec((1, d), lambda i, j: (0, 0)),
                  pl.BlockSpec((lower_bounds.shape[0], tn), lambda i, j: (0, j)),
                  wspec(0), wspec(1), wspec(2), wspec(3)],
        out_specs=[ospec] * 5,
        out_shape=[jax.ShapeDtypeStruct((m, d), BF16), jax.ShapeDtypeStruct((m, d), F32),
                   jax.ShapeDtypeStruct((m, d), BF16), jax.ShapeDtypeStruct((m, d), BF16),
                   jax.ShapeDtypeStruct((m, d), BF16)],
        scratch_shapes=[pltpu.VMEM((tm, d), BF16)],
        compiler_params=_params("arbitrary", "arbitrary"),
        name="hgrn_in",
    )(h, norm_g.reshape(1, d), lower_bounds, w_in, w_in, w_in, w_in)


def _gla_tables(c):
    r = np.arange(c)
    j = r[None, :]
    mats = [j <= r[:, None]]
    small = (4, 2, 1)
    for m in small:
        seg, pos = r // (2 * m), r % (2 * m)
        mid = (seg * 2 * m + m)[:, None]
        second = (pos >= m)[:, None]
        mats.append(np.where(second, (j >= mid) & (j <= r[:, None]), (j > r[:, None]) & (j <= mid - 1)))
    d = np.concatenate(mats, 0).astype(np.float32)
    dmat = np.concatenate([d, d, d], axis=1)
    lev = np.full((c, c), -1, np.int32)
    for li, m in enumerate(small):
        seg, pos = r // (2 * m), r % (2 * m)
        ok = (seg[:, None] == seg[None, :]) & (pos >= m)[:, None] & (pos < m)[None, :]
        lev[ok] = li
    big = []
    m = c // 2
    while m >= SUBLANES:
        t = np.concatenate([np.arange(s0 + m, s0 + 2 * m) for s0 in range(0, c, 2 * m)])
        ok = (t[:, None] // (2 * m) == r[None, :] // (2 * m)) & ((r % (2 * m)) < m)[None, :]
        big.append(ok.astype(np.float32))
        m //= 2
    return dmat, lev, np.stack(big)


def _gla_kernel(chunk, q_ref, k_ref, v_ref, g_ref, gt_ref, hg_ref, dmat_ref, lev_ref, big_ref,
                o_ref, st_ref):
    @pl.when(pl.program_id(2) == 0)
    def _():
        st_ref[...] = jnp.zeros_like(st_ref)

    c = chunk
    nc = q_ref.shape[0] // c
    lev = lev_ref[...]
    hg = hg_ref[...]

    g = g_ref[...]
    g_hi = g.astype(BF16)
    r1 = g - g_hi.astype(F32)
    g_mid = r1.astype(BF16)
    g_lo = (r1 - g_mid.astype(F32)).astype(BF16)
    lanes = lambda x: jnp.concatenate([x[i * c:(i + 1) * c] for i in range(nc)], axis=1)
    e_all = _dot(dmat_ref[...], jnp.concatenate([lanes(g_hi), lanes(g_mid), lanes(g_lo)], axis=0))

    small_p, big_p, big_tgt = [], [], []
    for ci in range(nc):
        sl = pl.ds(ci * c, c)
        cols = slice(ci * HEAD_DIM, (ci + 1) * HEAD_DIM)
        b = e_all[0:c, cols]
        q = q_ref[sl, :].astype(F32)
        k = k_ref[sl, :].astype(F32)
        ps = []
        for li in range(3):
            w = jnp.exp(e_all[(li + 1) * c:(li + 2) * c, cols])
            ps.append(_dot_nt((q * w).astype(BF16), (k * w).astype(BF16)))
        small_p.append(ps)
        ps, tg = [], []
        m = c // 2
        while m >= SUBLANES:
            qs, ks, tgt = [], [], []
            for s0 in range(0, c, 2 * m):
                ref = b[s0 + m - 1:s0 + m, :]
                qs.append(q[s0 + m:s0 + 2 * m] * jnp.exp(b[s0 + m:s0 + 2 * m] - ref))
                ks.append(k[s0:s0 + m] * jnp.exp(ref - b[s0:s0 + m]))
                ks.append(k[s0 + m:s0 + 2 * m])
                tgt.extend(range((s0 + m) // SUBLANES, (s0 + 2 * m) // SUBLANES))
            ps.append(_dot_nt(jnp.concatenate(qs, 0).astype(BF16), jnp.concatenate(ks, 0).astype(BF16)))
            tg.append(tgt)
            m //= 2
        big_p.append(ps)
        big_tgt.append(tg)

    intra, qbs, upds, dcols = [], [], [], []
    for ci in range(nc):
        sl = pl.ds(ci * c, c)
        cols = slice(ci * HEAD_DIM, (ci + 1) * HEAD_DIM)
        b = e_all[0:c, cols]
        q = q_ref[sl, :].astype(F32)
        k = k_ref[sl, :].astype(F32)
        v = v_ref[sl, :]
        rows = [jnp.zeros((SUBLANES, c), F32) for _ in range(c // SUBLANES)]
        for li, p in enumerate(small_p[ci]):
            p = jnp.where(lev == li, p, 0.0)
            rows = [rw + p[i * SUBLANES:(i + 1) * SUBLANES] for i, rw in enumerate(rows)]
        for li, (p, tgt) in enumerate(zip(big_p[ci], big_tgt[ci])):
            p = p * big_ref[li]
            for n, i in enumerate(tgt):
                rows[i] = rows[i] + p[n * SUBLANES:(n + 1) * SUBLANES]
        scores = jnp.concatenate(rows, 0).astype(BF16)
        wb = jnp.exp(b)
        we = jnp.exp(b[c - 1:c, :] - b)
        intra.append(_dot(scores, v) + jnp.sum(q * k, axis=-1, keepdims=True) * v.astype(F32))
        qbs.append((q * wb).astype(BF16))
        upds.append(_dot_tn((k * we).astype(BF16), v))
        dcols.append(jnp.broadcast_to(wb[c - 1:c, :], (SUBLANES, HEAD_DIM)).T[:, :1])

    st = st_ref[...]
    states = []
    for ci in range(nc):
        states.append(st.astype(BF16))
        st = st * dcols[ci] + upds[ci]
    st_ref[...] = st

    for ci in range(nc):
        sl = pl.ds(ci * c, c)
        o = _dot(qbs[ci], states[ci]) + intra[ci]
        o_ref[sl, :] = (_rms(o, hg) * gt_ref[sl, :].astype(F32)).astype(BF16)


def _gla(q, k, v, logf, gate, head_gain, batch, seq, ts):
    m, d = q.shape
    heads = d // HEAD_DIM
    ns = seq // ts
    dmat, lev, big = _gla_tables(GLA_CHUNK)
    spec = pl.BlockSpec((ts, HEAD_DIM), lambda b, h, s: (b * ns + s, h))
    const = lambda a: pl.BlockSpec(a.shape, lambda b, h, s: (0,) * a.ndim)
    return pl.pallas_call(
        functools.partial(_gla_kernel, GLA_CHUNK),
        grid=(batch, heads, ns),
        in_specs=[spec, spec, spec, spec, spec,
                  pl.BlockSpec((1, HEAD_DIM), lambda b, h, s: (0, h)),
                  const(dmat), const(lev), const(big)],
        out_specs=spec,
        out_shape=jax.ShapeDtypeStruct((m, d), BF16),
        scratch_shapes=[pltpu.VMEM((HEAD_DIM, HEAD_DIM), F32)],
        compiler_params=_params("arbitrary", "arbitrary", "arbitrary"),
        name="gla",
    )(q, k, v, logf, gate, head_gain.reshape(1, d), jnp.asarray(dmat, BF16), jnp.asarray(lev),
      jnp.asarray(big))


def _proj_res_kernel(h_ref, a_ref, w_ref, o_ref, wb_ref):
    @pl.when(pl.program_id(0) == 0)
    def _():
        wb_ref[...] = w_ref[...].astype(BF16)

    o_ref[...] = h_ref[...] + _dot(a_ref[...], wb_ref[...])


def _resident_spec(w, layer):
    return pl.BlockSpec((None,) + w.shape[1:], lambda i: (layer, 0, 0), pipeline_mode=pl.Buffered(1))


def _proj_res(h, a, w, layer, tm):
    m, d = h.shape
    kdim = a.shape[1]
    return pl.pallas_call(
        _proj_res_kernel,
        grid=(m // tm,),
        in_specs=[pl.BlockSpec((tm, d), lambda i: (i, 0)),
                  pl.BlockSpec((tm, kdim), lambda i: (i, 0)),
                  _resident_spec(w, layer)],
        out_specs=pl.BlockSpec((tm, d), lambda i: (i, 0)),
        out_shape=jax.ShapeDtypeStruct((m, d), F32),
        scratch_shapes=[pltpu.VMEM((kdim, d), BF16)],
        compiler_params=_params("arbitrary"),
        name="proj_res",
    )(h, a, w)


def _mlp_kernel(h_ref, ng_ref, w1_ref, w2_ref, o_ref, xn_ref):
    @pl.when(pl.program_id(1) == 0)
    def _():
        x = h_ref[...]
        xn_ref[...] = _rms(x, ng_ref[...]).astype(BF16)
        o_ref[...] = x

    t = jnp.square(jnp.maximum(_wdot(xn_ref[...], w1_ref), 0.0)).astype(BF16)
    o_ref[...] += _wdot(t, w2_ref)


def _mlp(h, norm_g, w1, w2, layer, tm, tf):
    m, d = h.shape
    ff = w1.shape[-1]
    return pl.pallas_call(
        _mlp_kernel,
        grid=(m // tm, ff // tf),
        in_specs=[pl.BlockSpec((tm, d), lambda i, j: (i, 0), pipeline_mode=pl.Buffered(1)),
                  pl.BlockSpec((1, d), lambda i, j: (0, 0)),
                  _layer_spec(layer, (d, tf), lambda i, j: (0, j)),
                  _layer_spec(layer, (tf, d), lambda i, j: (j, 0))],
        out_specs=pl.BlockSpec((tm, d), lambda i, j: (i, 0)),
        out_shape=jax.ShapeDtypeStruct((m, d), F32),
        scratch_shapes=[pltpu.VMEM((tm, d), BF16)],
        compiler_params=_params("arbitrary", "arbitrary"),
        name="mlp",
    )(h, norm_g.reshape(1, d), w1, w2)


def _norm_proj_kernel(head_norm, block_mean, *refs):
    h_ref, ng_ref, hn_ref, w_ref = refs[:4]
    outs = refs[4:-1]
    wb_ref = refs[-1]

    @pl.when(pl.program_id(0) == 0)
    def _():
        wb_ref[...] = w_ref[...].astype(BF16)

    tm = h_ref.shape[0]
    sub = MOBA_BLOCK if tm % MOBA_BLOCK == 0 else tm
    for r0 in range(0, tm, sub):
        rows = slice(r0, r0 + sub)
        y = _dot(_rms(h_ref[rows, :], ng_ref[...]).astype(BF16), wb_ref[...])
        if not head_norm:
            outs[0][rows, :] = y.astype(BF16)
            continue
        for hh, yh in enumerate(_head_rms(y, hn_ref[...])):
            cols = slice(hh * HEAD_DIM, (hh + 1) * HEAD_DIM)
            outs[0][rows, cols] = yh.astype(BF16)
            if block_mean:
                outs[1][0, r0 // MOBA_BLOCK:(r0 + sub) // MOBA_BLOCK, cols] = (
                    jnp.sum(yh.reshape(sub // MOBA_BLOCK, MOBA_BLOCK, HEAD_DIM), axis=1) * (1.0 / MOBA_BLOCK))


def _norm_proj(h, norm_g, head_g, w, wspec, tm, head_norm, block_mean, name):
    m, d = h.shape
    n = wspec.block_shape[-1]
    nblk = tm // MOBA_BLOCK
    out_specs = [pl.BlockSpec((tm, n), lambda i: (i, 0))]
    out_shape = [jax.ShapeDtypeStruct((m, n), BF16)]
    if block_mean:
        out_specs.append(pl.BlockSpec((1, nblk, n), lambda i: (i, 0, 0)))
        out_shape.append(jax.ShapeDtypeStruct((m // tm, nblk, n), F32))
    return pl.pallas_call(
        functools.partial(_norm_proj_kernel, head_norm, block_mean),
        grid=(m // tm,),
        in_specs=[pl.BlockSpec((tm, d), lambda i: (i, 0)),
                  pl.BlockSpec((1, d), lambda i: (0, 0)),
                  pl.BlockSpec((1, HEAD_DIM), lambda i: (0, 0)),
                  wspec],
        out_specs=out_specs,
        out_shape=out_shape,
        scratch_shapes=[pltpu.VMEM((d, n), BF16)],
        compiler_params=_params("arbitrary"),
        name=name,
    )(h, norm_g.reshape(1, d), head_g.reshape(1, HEAD_DIM), w)


def _moba_kernel(nblk, q_ref, k_ref, v_ref, km_ref, slope_ref, o_ref, vt_ref, bias_ref):
    blk = MOBA_BLOCK
    grp = blk // SUBLANES
    scale = LOG2E * HEAD_DIM ** -0.5
    for n in range(nblk):
        vt_ref[:, n * blk:(n + 1) * blk] = v_ref[n * blk:(n + 1) * blk, :].astype(F32).T.astype(BF16)
    slope = slope_ref[0][:, :1]
    kmean = km_ref[0].astype(BF16)
    t_idx = lax.broadcasted_iota(jnp.int32, (blk, blk), 1)
    s_idx = lax.broadcasted_iota(jnp.int32, (blk, blk), 0)
    dist0 = (t_idx - s_idx).astype(F32)
    bias_ref[0] = jnp.where(dist0 >= 0.0, (LOG2E * slope) * dist0, -NEG_BIG)
    for dlt in range(1, nblk):
        bias_ref[dlt] = (LOG2E * slope) * (dist0 + float(dlt * blk))
    n_idx = lax.broadcasted_iota(jnp.int32, (nblk, blk), 0)

    for i in range(nblk):
        qi = q_ref[i * blk:(i + 1) * blk, :]
        sel = None
        if i > MOBA_TOPK:
            gm = jnp.where(n_idx < i, _dot_nt(kmean, qi), NEG_BIG)
            rank = jnp.zeros((nblk, blk), F32)
            for mrow in range(nblk):
                gr = gm[mrow:mrow + 1, :]
                rank = rank + ((gr > gm) | ((gr == gm) & (mrow < n_idx))).astype(F32)
            sel = ((rank < MOBA_TOPK) & (n_idx < i)).astype(F32)
        s_all = _dot_nt(k_ref[0:(i + 1) * blk, :], qi)
        s = []
        for j in range(i + 1):
            sj = s_all[j * blk:(j + 1) * blk].reshape(grp, SUBLANES, blk) * scale \
                - bias_ref[i - j].reshape(grp, SUBLANES, blk)
            if sel is not None and j < i:
                keep = jnp.broadcast_to(sel[j:j + 1, :], (SUBLANES, blk)) > 0.0
                sj = jnp.where(keep[None], sj, NEG_BIG)
            s.append(sj)
        m8 = functools.reduce(jnp.maximum, [jnp.max(sj, axis=0) for sj in s])
        m = jnp.broadcast_to(jnp.max(m8, axis=0, keepdims=True), (SUBLANES, blk))
        p = [jnp.exp2(sj - m[None]) for sj in s]
        l8 = functools.reduce(lambda a, b: a + b, [jnp.sum(pj, axis=0) for pj in p])
        l = jnp.sum(l8, axis=0, keepdims=True)
        pcat = jnp.concatenate([pj.reshape(blk, blk).astype(BF16) for pj in p], axis=0)
        acc = _dot(vt_ref[:, 0:(i + 1) * blk], pcat)
        o_ref[i * blk:(i + 1) * blk, :] = (acc / l).T.astype(BF16)


def _moba(q, k, v, kmean, batch, seq):
    m, d = q.shape
    heads = d // HEAD_DIM
    nblk = seq // MOBA_BLOCK
    slopes = 2.0 ** (-8.0 * jnp.arange(1, heads + 1, dtype=F32) / heads)
    slopes = jnp.broadcast_to(slopes[:, None, None], (heads, 1, HEAD_DIM))
    spec = pl.BlockSpec((seq, HEAD_DIM), lambda b, h: (b, h))
    return pl.pallas_call(
        functools.partial(_moba_kernel, nblk),
        grid=(batch, heads),
        in_specs=[spec, spec, spec,
                  pl.BlockSpec((1, nblk, HEAD_DIM), lambda b, h: (b, 0, h)),
                  pl.BlockSpec((1, 1, HEAD_DIM), lambda b, h: (h, 0, 0))],
        out_specs=spec,
        out_shape=jax.ShapeDtypeStruct((m, d), BF16),
        scratch_shapes=[pltpu.VMEM((HEAD_DIM, seq), BF16),
                        pltpu.VMEM((nblk, MOBA_BLOCK, MOBA_BLOCK), F32)],
        compiler_params=_params("arbitrary", "arbitrary"),
        name="moba_attn",
    )(q, k, v, kmean, slopes)


def _tile(n, pref):
    return pref if n % pref == 0 else n


def kernel(x, a_norm, a_w_in, a_head_norm, a_w_out, lower_bounds, kv_norm, w_kv, k_norm,
           b_norm, b_w_q, b_q_norm, b_w_o, mlp_norm, mlp_w1, mlp_w2):
    batch, seq, d = x.shape
    n_a = a_w_in.shape[0]
    n_b = b_w_q.shape[0]
    assert seq % MOBA_BLOCK == 0 and d % HEAD_DIM == 0
    m = batch * seq
    tm = _tile(m, 1024)
    ts = _tile(seq, 512)
    tr = _tile(m, 512)

    h = x.reshape(m, d)
    kb = vb = kmean = None
    for l in range(n_a + n_b):
        if l < n_a:
            q, logf, k, v, gate = _hgrn_in(h, a_norm[l], lower_bounds, a_w_in, l, tm, 256)
            a = _gla(q, k, v, logf, gate, a_head_norm[l], batch, seq, ts)
            h = _proj_res(h, a, a_w_out, l, tr)
        else:
            if l == n_a:
                half = lambda c: pl.BlockSpec((d, d), lambda i: (0, c), pipeline_mode=pl.Buffered(1))
                kb, kmean = _norm_proj(h, kv_norm, k_norm, w_kv, half(0), tr, True, True, "shared_k")
                vb, = _norm_proj(h, kv_norm, k_norm, w_kv, half(1), tr, False, False, "shared_v")
                kmean = kmean.reshape(batch, seq // MOBA_BLOCK, d)
            j = l - n_a
            q, = _norm_proj(h, b_norm[j], b_q_norm[j], b_w_q, _resident_spec(b_w_q, j), tr, True, False,
                            "moba_q")
            a = _moba(q, kb, vb, kmean, batch, seq)
            h = _proj_res(h, a, b_w_o, j, tr)
        h = _mlp(h, mlp_norm[l], mlp_w1, mlp_w2, l, tm, 512)
    return h.reshape(batch, seq, d)
```

```python
import functools

import numpy as np
import jax
import jax.numpy as jnp
from jax import lax
from jax.experimental import pallas as pl
from jax.experimental.pallas import tpu as pltpu

F32 = jnp.float32
BF16 = jnp.bfloat16

HEAD_DIM = 128
MOBA_BLOCK = 256
MOBA_TOPK = 3
GLA_CHUNK = 64
EPS = 1e-6
NEG_BIG = -1e30
LB_FLOOR = 1e-30
LOG2E = 1.4426950408889634
SUBLANES = 8

V7X_VMEM_LIMIT_BYTES = 56 * 1024 * 1024


def _params(*sem):
    return pltpu.CompilerParams(dimension_semantics=sem, vmem_limit_bytes=V7X_VMEM_LIMIT_BYTES)


def _dot(a, b):
    return jnp.dot(a, b, preferred_element_type=F32)


def _wdot(a, w_ref):
    return jnp.dot(a, w_ref[...].astype(BF16), preferred_element_type=F32)


def _dot_nt(a, b):
    return lax.dot_general(a, b, (((1,), (1,)), ((), ())), preferred_element_type=F32)


def _dot_tn(a, b):
    return lax.dot_general(a, b, (((0,), (0,)), ((), ())), preferred_element_type=F32)


def _layer_spec(layer, block, index):
    return pl.BlockSpec((None,) + block, lambda i, j: (layer,) + index(i, j))


def _rms(x, g):
    return x * lax.rsqrt(jnp.mean(x * x, axis=-1, keepdims=True) + EPS) * g


def _silu(x):
    return x * jax.nn.sigmoid(x)


def _head_rms(x, g):
    outs = []
    for hh in range(x.shape[1] // HEAD_DIM):
        outs.append(_rms(x[:, hh * HEAD_DIM:(hh + 1) * HEAD_DIM], g))
    return outs


def _hgrn_in_kernel(layer, h_ref, ng_ref, lb_ref, wq_ref, wf_ref, wi_ref, wg_ref,
                    q_ref, lf_ref, k_ref, v_ref, gt_ref, xn_ref):
    @pl.when(pl.program_id(1) == 0)
    def _():
        xn_ref[...] = _rms(h_ref[...], ng_ref[...]).astype(BF16)

    xn = xn_ref[...]
    pf = _wdot(xn, wf_ref)
    pq = _wdot(xn, wq_ref)
    pg = _wdot(xn, wg_ref)
    pv = _wdot(xn, wi_ref)

    lbr = lb_ref[...]
    rows = [lbr[r:r + 1, :] for r in range(lbr.shape[0])]
    mx = functools.reduce(jnp.maximum, rows)
    ex = [jnp.exp(r - mx) for r in rows]
    den = functools.reduce(lambda a, b: a + b, ex)
    p = [e / den for e in ex]
    lb = functools.reduce(lambda a, b: a + b, p[:layer + 1]) - p[0]

    e = jnp.exp(-jnp.abs(pf))
    r = 1.0 / (1.0 + e)
    er = e * r
    pos = pf >= 0.0
    one_m = 1.0 - lb
    lf_ref[...] = jnp.log(jnp.maximum(lb, LB_FLOOR) + one_m * jnp.where(pos, r, er))
    k_ref[...] = (one_m * jnp.where(pos, er, r)).astype(BF16)
    q_ref[...] = _silu(pq).astype(BF16)
    gt_ref[...] = _silu(pg).astype(BF16)
    v_ref[...] = pv.astype(BF16)


def _hgrn_in(h, norm_g, lower_bounds, w_in, layer, tm, tn):
    m, d = h.shape
    nj = d // tn
    wspec = lambda g: _layer_spec(layer, (d, tn), lambda i, j: (0, j + g * nj))
    ospec = pl.BlockSpec((tm, tn), lambda i, j: (i, j))
    return pl.pallas_call(
        functools.partial(_hgrn_in_kernel, layer),
        grid=(m // tm, nj),
        in_specs=[pl.BlockSpec((tm, d), lambda i, j: (i, 0)),
                  pl.BlockSpec((1, d), lambda i, j: (0, 0)),
                  pl.BlockSpec((lower_bounds.shape[0], tn), lambda i, j: (0, j)),
                  wspec(0), wspec(1), wspec(2), wspec(3)],
        out_specs=[ospec] * 5,
        out_shape=[jax.ShapeDtypeStruct((m, d), BF16), jax.ShapeDtypeStruct((m, d), F32),
                   jax.ShapeDtypeStruct((m, d), BF16), jax.ShapeDtypeStruct((m, d), BF16),
                   jax.ShapeDtypeStruct((m, d), BF16)],
        scratch_shapes=[pltpu.VMEM((tm, d), BF16)],
        compiler_params=_params("arbitrary", "arbitrary"),
        name="hgrn_in",
    )(h, norm_g.reshape(1, d), lower_bounds, w_in, w_in, w_in, w_in)


def _gla_tables(c):
    r = np.arange(c)
    j = r[None, :]
    mats = [j <= r[:, None]]
    small = (4, 2, 1)
    for m in small:
        seg, pos = r // (2 * m), r % (2 * m)
        mid = (seg * 2 * m + m)[:, None]
        second = (pos >= m)[:, None]
        mats.append(np.where(second, (j >= mid) & (j <= r[:, None]), (j > r[:, None]) & (j <= mid - 1)))
    d = np.concatenate(mats, 0).astype(np.float32)
    dmat = np.concatenate([d, d, d], axis=1)
    lev = np.full((c, c), -1, np.int32)
    for li, m in enumerate(small):
        seg, pos = r // (2 * m), r % (2 * m)
        ok = (seg[:, None] == seg[None, :]) & (pos >= m)[:, None] & (pos < m)[None, :]
        lev[ok] = li
    big = []
    m = c // 2
    while m >= SUBLANES:
        t = np.concatenate([np.arange(s0 + m, s0 + 2 * m) for s0 in range(0, c, 2 * m)])
        ok = (t[:, None] // (2 * m) == r[None, :] // (2 * m)) & ((r % (2 * m)) < m)[None, :]
        big.append(ok.astype(np.float32))
        m //= 2
    return dmat, lev, np.stack(big)


def _gla_kernel(chunk, q_ref, k_ref, v_ref, g_ref, gt_ref, hg_ref, dmat_ref, lev_ref, big_ref,
                o_ref, st_ref):
    @pl.when(pl.program_id(2) == 0)
    def _():
        st_ref[...] = jnp.zeros_like(st_ref)

    c = chunk
    nc = q_ref.shape[0] // c
    lev = lev_ref[...]
    hg = hg_ref[...]

    g = g_ref[...]
    g_hi = g.astype(BF16)
    r1 = g - g_hi.astype(F32)
    g_mid = r1.astype(BF16)
    g_lo = (r1 - g_mid.astype(F32)).astype(BF16)
    lanes = lambda x: jnp.concatenate([x[i * c:(i + 1) * c] for i in range(nc)], axis=1)
    e_all = _dot(dmat_ref[...], jnp.concatenate([lanes(g_hi), lanes(g_mid), lanes(g_lo)], axis=0))

    small_p, big_p, big_tgt = [], [], []
    for ci in range(nc):
        sl = pl.ds(ci * c, c)
        cols = slice(ci * HEAD_DIM, (ci + 1) * HEAD_DIM)
        b = e_all[0:c, cols]
        q = q_ref[sl, :].astype(F32)
        k = k_ref[sl, :].astype(F32)
        ps = []
        for li in range(3):
            w = jnp.exp(e_all[(li + 1) * c:(li + 2) * c, cols])
            ps.append(_dot_nt((q * w).astype(BF16), (k * w).astype(BF16)))
        small_p.append(ps)
        ps, tg = [], []
        m = c // 2
        while m >= SUBLANES:
            qs, ks, tgt = [], [], []
            for s0 in range(0, c, 2 * m):
                ref = b[s0 + m - 1:s0 + m, :]
                qs.append(q[s0 + m:s0 + 2 * m] * jnp.exp(b[s0 + m:s0 + 2 * m] - ref))
                ks.append(k[s0:s0 + m] * jnp.exp(ref - b[s0:s0 + m]))
                ks.append(k[s0 + m:s0 + 2 * m])
                tgt.extend(range((s0 + m) // SUBLANES, (s0 + 2 * m) // SUBLANES))
            ps.append(_dot_nt(jnp.concatenate(qs, 0).astype(BF16), jnp.concatenate(ks, 0).astype(BF16)))
            tg.append(tgt)
            m //= 2
        big_p.append(ps)
        big_tgt.append(tg)

    intra, qbs, upds, dcols = [], [], [], []
    for ci in range(nc):
        sl = pl.ds(ci * c, c)
        cols = slice(ci * HEAD_DIM, (ci + 1) * HEAD_DIM)
        b = e_all[0:c, cols]
        q = q_ref[sl, :].astype(F32)
        k = k_ref[sl, :].astype(F32)
        v = v_ref[sl, :]
        rows = [jnp.zeros((SUBLANES, c), F32) for _ in range(c // SUBLANES)]
        for li, p in enumerate(small_p[ci]):
            p = jnp.where(lev == li, p, 0.0)
            rows = [rw + p[i * SUBLANES:(i + 1) * SUBLANES] for i, rw in enumerate(rows)]
        for li, (p, tgt) in enumerate(zip(big_p[ci], big_tgt[ci])):
            p = p * big_ref[li]
            for n, i in enumerate(tgt):
                rows[i] = rows[i] + p[n * SUBLANES:(n + 1) * SUBLANES]
        scores = jnp.concatenate(rows, 0).astype(BF16)
        wb = jnp.exp(b)
        we = jnp.exp(b[c - 1:c, :] - b)
        intra.append(_dot(scores, v) + jnp.sum(q * k, axis=-1, keepdims=True) * v.astype(F32))
        qbs.append((q * wb).astype(BF16))
        upds.append(_dot_tn((k * we).astype(BF16), v))
        dcols.append(jnp.broadcast_to(wb[c - 1:c, :], (SUBLANES, HEAD_DIM)).T[:, :1])

    st = st_ref[...]
    states = []
    for ci in range(nc):
        states.append(st.astype(BF16))
        st = st * dcols[ci] + upds[ci]
    st_ref[...] = st

    for ci in range(nc):
        sl = pl.ds(ci * c, c)
        o = _dot(qbs[ci], states[ci]) + intra[ci]
        o_ref[sl, :] = (_rms(o, hg) * gt_ref[sl, :].astype(F32)).astype(BF16)


def _gla(q, k, v, logf, gate, head_gain, batch, seq, ts):
    m, d = q.shape
    heads = d // HEAD_DIM
    ns = seq // ts
    dmat, lev, big = _gla_tables(GLA_CHUNK)
    spec = pl.BlockSpec((ts, HEAD_DIM), lambda b, h, s: (b * ns + s, h))
    const = lambda a: pl.BlockSpec(a.shape, lambda b, h, s: (0,) * a.ndim)
    return pl.pallas_call(
        functools.partial(_gla_kernel, GLA_CHUNK),
        grid=(batch, heads, ns),
        in_specs=[spec, spec, spec, spec, spec,
                  pl.BlockSpec((1, HEAD_DIM), lambda b, h, s: (0, h)),
                  const(dmat), const(lev), const(big)],
        out_specs=spec,
        out_shape=jax.ShapeDtypeStruct((m, d), BF16),
        scratch_shapes=[pltpu.VMEM((HEAD_DIM, HEAD_DIM), F32)],
        compiler_params=_params("arbitrary", "arbitrary", "arbitrary"),
        name="gla",
    )(q, k, v, logf, gate, head_gain.reshape(1, d), jnp.asarray(dmat, BF16), jnp.asarray(lev),
      jnp.asarray(big))


def _proj_res_kernel(h_ref, a_ref, w_ref, o_ref, wb_ref):
    @pl.when(pl.program_id(0) == 0)
    def _():
        wb_ref[...] = w_ref[...].astype(BF16)

    o_ref[...] = h_ref[...] + _dot(a_ref[...], wb_ref[...])


def _resident_spec(w, layer):
    return pl.BlockSpec((None,) + w.shape[1:], lambda i: (layer, 0, 0), pipeline_mode=pl.Buffered(1))


def _proj_res(h, a, w, layer, tm):
    m, d = h.shape
    kdim = a.shape[1]
    return pl.pallas_call(
        _proj_res_kernel,
        grid=(m // tm,),
        in_specs=[pl.BlockSpec((tm, d), lambda i: (i, 0)),
                  pl.BlockSpec((tm, kdim), lambda i: (i, 0)),
                  _resident_spec(w, layer)],
        out_specs=pl.BlockSpec((tm, d), lambda i: (i, 0)),
        out_shape=jax.ShapeDtypeStruct((m, d), F32),
        scratch_shapes=[pltpu.VMEM((kdim, d), BF16)],
        compiler_params=_params("arbitrary"),
        name="proj_res",
    )(h, a, w)


def _mlp_kernel(h_ref, ng_ref, w1_ref, w2_ref, o_ref, xn_ref):
    @pl.when(pl.program_id(1) == 0)
    def _():
        x = h_ref[...]
        xn_ref[...] = _rms(x, ng_ref[...]).astype(BF16)
        o_ref[...] = x

    t = jnp.square(jnp.maximum(_wdot(xn_ref[...], w1_ref), 0.0)).astype(BF16)
    o_ref[...] += _wdot(t, w2_ref)


def _mlp(h, norm_g, w1, w2, layer, tm, tf):
    m, d = h.shape
    ff = w1.shape[-1]
    return pl.pallas_call(
        _mlp_kernel,
        grid=(m // tm, ff // tf),
        in_specs=[pl.BlockSpec((tm, d), lambda i, j: (i, 0)),
                  pl.BlockSpec((1, d), lambda i, j: (0, 0)),
                  _layer_spec(layer, (d, tf), lambda i, j: (0, j)),
                  _layer_spec(layer, (tf, d), lambda i, j: (j, 0))],
        out_specs=pl.BlockSpec((tm, d), lambda i, j: (i, 0)),
        out_shape=jax.ShapeDtypeStruct((m, d), F32),
        scratch_shapes=[pltpu.VMEM((tm, d), BF16)],
        compiler_params=_params("arbitrary", "arbitrary"),
        name="mlp",
    )(h, norm_g.reshape(1, d), w1, w2)


def _norm_proj_kernel(head_norm, block_mean, *refs):
    h_ref, ng_ref, hn_ref, w_ref = refs[:4]
    outs = refs[4:-1]
    wb_ref = refs[-1]

    @pl.when(pl.program_id(0) == 0)
    def _():
        wb_ref[...] = w_ref[...].astype(BF16)

    tm = h_ref.shape[0]
    sub = MOBA_BLOCK if tm % MOBA_BLOCK == 0 else tm
    for r0 in range(0, tm, sub):
        rows = slice(r0, r0 + sub)
        y = _dot(_rms(h_ref[rows, :], ng_ref[...]).astype(BF16), wb_ref[...])
        if not head_norm:
            outs[0][rows, :] = y.astype(BF16)
            continue
        for hh, yh in enumerate(_head_rms(y, hn_ref[...])):
            cols = slice(hh * HEAD_DIM, (hh + 1) * HEAD_DIM)
            outs[0][rows, cols] = yh.astype(BF16)
            if block_mean:
                outs[1][0, r0 // MOBA_BLOCK:(r0 + sub) // MOBA_BLOCK, cols] = (
                    jnp.sum(yh.reshape(sub // MOBA_BLOCK, MOBA_BLOCK, HEAD_DIM), axis=1) * (1.0 / MOBA_BLOCK))


def _norm_proj(h, norm_g, head_g, w, wspec, tm, head_norm, block_mean, name):
    m, d = h.shape
    n = wspec.block_shape[-1]
    nblk = tm // MOBA_BLOCK
    out_specs = [pl.BlockSpec((tm, n), lambda i: (i, 0))]
    out_shape = [jax.ShapeDtypeStruct((m, n), BF16)]
    if block_mean:
        out_specs.append(pl.BlockSpec((1, nblk, n), lambda i: (i, 0, 0)))
        out_shape.append(jax.ShapeDtypeStruct((m // tm, nblk, n), F32))
    return pl.pallas_call(
        functools.partial(_norm_proj_kernel, head_norm, block_mean),
        grid=(m // tm,),
        in_specs=[pl.BlockSpec((tm, d), lambda i: (i, 0)),
                  pl.BlockSpec((1, d), lambda i: (0, 0)),
                  pl.BlockSpec((1, HEAD_DIM), lambda i: (0, 0)),
                  wspec],
        out_specs=out_specs,
        out_shape=out_shape,
        scratch_shapes=[pltpu.VMEM((d, n), BF16)],
        compiler_params=_params("arbitrary"),
        name=name,
    )(h, norm_g.reshape(1, d), head_g.reshape(1, HEAD_DIM), w)


def _moba_kernel(nblk, q_ref, k_ref, v_ref, km_ref, slope_ref, o_ref, vt_ref, bias_ref):
    blk = MOBA_BLOCK
    grp = blk // SUBLANES
    scale = LOG2E * HEAD_DIM ** -0.5
    for n in range(nblk):
        vt_ref[:, n * blk:(n + 1) * blk] = v_ref[n * blk:(n + 1) * blk, :].astype(F32).T.astype(BF16)
    slope = slope_ref[0][:, :1]
    kmean = km_ref[0].astype(BF16)
    t_idx = lax.broadcasted_iota(jnp.int32, (blk, blk), 1)
    s_idx = lax.broadcasted_iota(jnp.int32, (blk, blk), 0)
    dist0 = (t_idx - s_idx).astype(F32)
    bias_ref[0] = jnp.where(dist0 >= 0.0, (LOG2E * slope) * dist0, -NEG_BIG)
    for dlt in range(1, nblk):
        bias_ref[dlt] = (LOG2E * slope) * (dist0 + float(dlt * blk))
    n_idx = lax.broadcasted_iota(jnp.int32, (nblk, blk), 0)

    def scores(i):
        qi = q_ref[i * blk:(i + 1) * blk, :]
        sel = None
        if i > MOBA_TOPK:
            gm = jnp.where(n_idx < i, _dot_nt(kmean, qi), NEG_BIG)
            rank = jnp.zeros((nblk, blk), F32)
            for mrow in range(nblk):
                gr = gm[mrow:mrow + 1, :]
                rank = rank + ((gr > gm) | ((gr == gm) & (mrow < n_idx))).astype(F32)
            sel = ((rank < MOBA_TOPK) & (n_idx < i)).astype(F32)
        return sel, _dot_nt(k_ref[0:(i + 1) * blk, :], qi)

    def softmax(i, sel, s_all):
        s = []
        for j in range(i + 1):
            sj = s_all[j * blk:(j + 1) * blk].reshape(grp, SUBLANES, blk) * scale \
                - bias_ref[i - j].reshape(grp, SUBLANES, blk)
            if sel is not None and j < i:
                keep = jnp.broadcast_to(sel[j:j + 1, :], (SUBLANES, blk)) > 0.0
                sj = jnp.where(keep[None], sj, NEG_BIG)
            s.append(sj)
        m8 = functools.reduce(jnp.maximum, [jnp.max(sj, axis=0) for sj in s])
        m = jnp.broadcast_to(jnp.max(m8, axis=0, keepdims=True), (SUBLANES, blk))
        p = [jnp.exp2(sj - m[None]) for sj in s]
        l8 = functools.reduce(lambda a, b: a + b, [jnp.sum(pj, axis=0) for pj in p])
        l = jnp.sum(l8, axis=0, keepdims=True)
        return jnp.concatenate([pj.reshape(blk, blk).astype(BF16) for pj in p], axis=0), l

    nxt = scores(0)
    for i in range(nblk):
        cur = nxt
        if i + 1 < nblk:
            nxt = scores(i + 1)
        pcat, l = softmax(i, *cur)
        acc = _dot(vt_ref[:, 0:(i + 1) * blk], pcat)
        o_ref[i * blk:(i + 1) * blk, :] = (acc / l).T.astype(BF16)


def _moba(q, k, v, kmean, batch, seq):
    m, d = q.shape
    heads = d // HEAD_DIM
    nblk = seq // MOBA_BLOCK
    slopes = 2.0 ** (-8.0 * jnp.arange(1, heads + 1, dtype=F32) / heads)
    slopes = jnp.broadcast_to(slopes[:, None, None], (heads, 1, HEAD_DIM))
    spec = pl.BlockSpec((seq, HEAD_DIM), lambda b, h: (b, h))
    return pl.pallas_call(
        functools.partial(_moba_kernel, nblk),
        grid=(batch, heads),
        in_specs=[spec, spec, spec,
                  pl.BlockSpec((1, nblk, HEAD_DIM), lambda b, h: (b, 0, h)),
                  pl.BlockSpec((1, 1, HEAD_DIM), lambda b, h: (h, 0, 0))],
        out_specs=spec,
        out_shape=jax.ShapeDtypeStruct((m, d), BF16),
        scratch_shapes=[pltpu.VMEM((HEAD_DIM, seq), BF16),
                        pltpu.VMEM((nblk, MOBA_BLOCK, MOBA_BLOCK), F32)],
        compiler_params=_params("arbitrary", "arbitrary"),
        name="moba_attn",
    )(q, k, v, kmean, slopes)


def _tile(n, pref):
    return pref if n % pref == 0 else n


def kernel(x, a_norm, a_w_in, a_head_norm, a_w_out, lower_bounds, kv_norm, w_kv, k_norm,
           b_norm, b_w_q, b_q_norm, b_w_o, mlp_norm, mlp_w1, mlp_w2):
    batch, seq, d = x.shape
    n_a = a_w_in.shape[0]
    n_b = b_w_q.shape[0]
    assert seq % MOBA_BLOCK == 0 and d % HEAD_DIM == 0
    m = batch * seq
    tm = _tile(m, 1024)
    ts = _tile(seq, 2048)
    tr = _tile(m, 512)

    h = x.reshape(m, d)
    kb = vb = kmean = None
    for l in range(n_a + n_b):
        if l < n_a:
            q, logf, k, v, gate = _hgrn_in(h, a_norm[l], lower_bounds, a_w_in, l, tm, 256)
            a = _gla(q, k, v, logf, gate, a_head_norm[l], batch, seq, ts)
            h = _proj_res(h, a, a_w_out, l, tr)
        else:
            if l == n_a:
                half = lambda c: pl.BlockSpec((d, d), lambda i: (0, c), pipeline_mode=pl.Buffered(1))
                kb, kmean = _norm_proj(h, kv_norm, k_norm, w_kv, half(0), tr, True, True, "shared_k")
                vb, = _norm_proj(h, kv_norm, k_norm, w_kv, half(1), tr, False, False, "shared_v")
                kmean = kmean.reshape(batch, seq // MOBA_BLOCK, d)
            j = l - n_a
            q, = _norm_proj(h, b_norm[j], b_q_norm[j], b_w_q, _resident_spec(b_w_q, j), tr, True, False,
                            "moba_q")
            a = _moba(q, kb, vb, kmean, batch, seq)
            h = _proj_res(h, a, b_w_o, j, tr)
        h = _mlp(h, mlp_norm[l], mlp_w1, mlp_w2, l, tm, 512)
    return h.reshape(batch, seq, d)
```

```python
import functools

import numpy as np
import jax
import jax.numpy as jnp
from jax import lax
from jax.experimental import pallas as pl
from jax.experimental.pallas import tpu as pltpu

F32 = jnp.float32
BF16 = jnp.bfloat16

HEAD_DIM = 128
MOBA_BLOCK = 256
MOBA_TOPK = 3
GLA_CHUNK = 64
EPS = 1e-6
NEG_BIG = -1e30
LB_FLOOR = 1e-30
LOG2E = 1.4426950408889634
SUBLANES = 8

V7X_VMEM_LIMIT_BYTES = 56 * 1024 * 1024


def _params(*sem):
    return pltpu.CompilerParams(dimension_semantics=sem, vmem_limit_bytes=V7X_VMEM_LIMIT_BYTES)


def _dot(a, b):
    return jnp.dot(a, b, preferred_element_type=F32)


def _wdot(a, w_ref):
    return jnp.dot(a, w_ref[...].astype(BF16), preferred_element_type=F32)


def _dot_nt(a, b):
    return lax.dot_general(a, b, (((1,), (1,)), ((), ())), preferred_element_type=F32)


def _dot_tn(a, b):
    return lax.dot_general(a, b, (((0,), (0,)), ((), ())), preferred_element_type=F32)


def _layer_spec(layer, block, index):
    return pl.BlockSpec((None,) + block, lambda i, j: (layer,) + index(i, j))


def _rms(x, g):
    return x * lax.rsqrt(jnp.mean(x * x, axis=-1, keepdims=True) + EPS) * g


def _silu(x):
    return x * jax.nn.sigmoid(x)


def _head_rms(x, g):
    outs = []
    for hh in range(x.shape[1] // HEAD_DIM):
        outs.append(_rms(x[:, hh * HEAD_DIM:(hh + 1) * HEAD_DIM], g))
    return outs


def _hgrn_in_kernel(layer, h_ref, ng_ref, lb_ref, wq_ref, wf_ref, wi_ref, wg_ref,
                    q_ref, lf_ref, k_ref, v_ref, gt_ref, xn_ref):
    @pl.when(pl.program_id(1) == 0)
    def _():
        xn_ref[...] = _rms(h_ref[...], ng_ref[...]).astype(BF16)

    xn = xn_ref[...]
    pf = _wdot(xn, wf_ref)
    pq = _wdot(xn, wq_ref)
    pg = _wdot(xn, wg_ref)
    pv = _wdot(xn, wi_ref)

    lbr = lb_ref[...]
    rows = [lbr[r:r + 1, :] for r in range(lbr.shape[0])]
    mx = functools.reduce(jnp.maximum, rows)
    ex = [jnp.exp(r - mx) for r in rows]
    den = functools.reduce(lambda a, b: a + b, ex)
    p = [e / den for e in ex]
    lb = functools.reduce(lambda a, b: a + b, p[:layer + 1]) - p[0]

    e = jnp.exp(-jnp.abs(pf))
    r = 1.0 / (1.0 + e)
    er = e * r
    pos = pf >= 0.0
    one_m = 1.0 - lb
    lf_ref[...] = jnp.log(jnp.maximum(lb, LB_FLOOR) + one_m * jnp.where(pos, r, er))
    k_ref[...] = (one_m * jnp.where(pos, er, r)).astype(BF16)
    q_ref[...] = _silu(pq).astype(BF16)
    gt_ref[...] = _silu(pg).astype(BF16)
    v_ref[...] = pv.astype(BF16)


def _hgrn_in(h, norm_g, lower_bounds, w_in, layer, tm, tn):
    m, d = h.shape
    nj = d // tn
    wspec = lambda g: _layer_spec(layer, (d, tn), lambda i, j: (0, j + g * nj))
    ospec = pl.BlockSpec((tm, tn), lambda i, j: (i, j))
    return pl.pallas_call(
        functools.partial(_hgrn_in_kernel, layer),
        grid=(m // tm, nj),
        in_specs=[pl.BlockSpec((tm, d), lambda i, j: (i, 0)),
                  pl.BlockSpec((1, d), lambda i, j: (0, 0)),
                  pl.BlockSpec((lower_bounds.shape[0], tn), lambda i, j: (0, j)),
                  wspec(0), wspec(1), wspec(2), wspec(3)],
        out_specs=[ospec] * 5,
        out_shape=[jax.ShapeDtypeStruct((m, d), BF16), jax.ShapeDtypeStruct((m, d), F32),
                   jax.ShapeDtypeStruct((m, d), BF16), jax.ShapeDtypeStruct((m, d), BF16),
                   jax.ShapeDtypeStruct((m, d), BF16)],
        scratch_shapes=[pltpu.VMEM((tm, d), BF16)],
        compiler_params=_params("arbitrary", "arbitrary"),
        name="hgrn_in",
    )(h, norm_g.reshape(1, d), lower_bounds, w_in, w_in, w_in, w_in)


def _gla_tables(c):
    r = np.arange(c)
    j = r[None, :]
    mats = [j <= r[:, None]]
    small = (4, 2, 1)
    for m in small:
        seg, pos = r // (2 * m), r % (2 * m)
        mid = (seg * 2 * m + m)[:, None]
        second = (pos >= m)[:, None]
        mats.append(np.where(second, (j >= mid) & (j <= r[:, None]), (j > r[:, None]) & (j <= mid - 1)))
    d = np.concatenate(mats, 0).astype(np.float32)
    dmat = np.concatenate([d, d, d], axis=1)
    lev = np.full((c, c), -1, np.int32)
    for li, m in enumerate(small):
        seg, pos = r // (2 * m), r % (2 * m)
        ok = (seg[:, None] == seg[None, :]) & (pos >= m)[:, None] & (pos < m)[None, :]
        lev[ok] = li
    big = []
    m = c // 2
    while m >= SUBLANES:
        t = np.concatenate([np.arange(s0 + m, s0 + 2 * m) for s0 in range(0, c, 2 * m)])
        ok = (t[:, None] // (2 * m) == r[None, :] // (2 * m)) & ((r % (2 * m)) < m)[None, :]
        big.append(ok.astype(np.float32))
        m //= 2
    return dmat, lev, np.stack(big)


def _gla_kernel(chunk, q_ref, k_ref, v_ref, g_ref, gt_ref, hg_ref, dmat_ref, lev_ref, big_ref,
                o_ref, st_ref):
    @pl.when(pl.program_id(2) == 0)
    def _():
        st_ref[...] = jnp.zeros_like(st_ref)

    c = chunk
    nc = q_ref.shape[0] // c
    lev = lev_ref[...]
    hg = hg_ref[...]

    g = g_ref[...] * LOG2E
    g_hi = g.astype(BF16)
    r1 = g - g_hi.astype(F32)
    g_mid = r1.astype(BF16)
    g_lo = (r1 - g_mid.astype(F32)).astype(BF16)
    lanes = lambda x: jnp.concatenate([x[i * c:(i + 1) * c] for i in range(nc)], axis=1)
    e_all = _dot(dmat_ref[...], jnp.concatenate([lanes(g_hi), lanes(g_mid), lanes(g_lo)], axis=0))

    small_p, big_p, big_tgt = [], [], []
    for ci in range(nc):
        sl = pl.ds(ci * c, c)
        cols = slice(ci * HEAD_DIM, (ci + 1) * HEAD_DIM)
        b = e_all[0:c, cols]
        q = q_ref[sl, :].astype(F32)
        k = k_ref[sl, :].astype(F32)
        ps = []
        for li in range(3):
            w = jnp.exp2(e_all[(li + 1) * c:(li + 2) * c, cols])
            ps.append(_dot_nt((q * w).astype(BF16), (k * w).astype(BF16)))
        small_p.append(ps)
        ps, tg = [], []
        m = c // 2
        while m >= SUBLANES:
            qs, ks, tgt = [], [], []
            for s0 in range(0, c, 2 * m):
                ref = b[s0 + m - 1:s0 + m, :]
                qs.append(q[s0 + m:s0 + 2 * m] * jnp.exp2(b[s0 + m:s0 + 2 * m] - ref))
                ks.append(k[s0:s0 + m] * jnp.exp2(ref - b[s0:s0 + m]))
                ks.append(k[s0 + m:s0 + 2 * m])
                tgt.extend(range((s0 + m) // SUBLANES, (s0 + 2 * m) // SUBLANES))
            ps.append(_dot_nt(jnp.concatenate(qs, 0).astype(BF16), jnp.concatenate(ks, 0).astype(BF16)))
            tg.append(tgt)
            m //= 2
        big_p.append(ps)
        big_tgt.append(tg)

    intra, qbs, upds, dcols = [], [], [], []
    for ci in range(nc):
        sl = pl.ds(ci * c, c)
        cols = slice(ci * HEAD_DIM, (ci + 1) * HEAD_DIM)
        b = e_all[0:c, cols]
        q = q_ref[sl, :].astype(F32)
        k = k_ref[sl, :].astype(F32)
        v = v_ref[sl, :]
        rows = [jnp.zeros((SUBLANES, c), F32) for _ in range(c // SUBLANES)]
        for li, p in enumerate(small_p[ci]):
            p = jnp.where(lev == li, p, 0.0)
            rows = [rw + p[i * SUBLANES:(i + 1) * SUBLANES] for i, rw in enumerate(rows)]
        for li, (p, tgt) in enumerate(zip(big_p[ci], big_tgt[ci])):
            p = p * big_ref[li]
            for n, i in enumerate(tgt):
                rows[i] = rows[i] + p[n * SUBLANES:(n + 1) * SUBLANES]
        scores = jnp.concatenate(rows, 0).astype(BF16)
        wb = jnp.exp2(b)
        we = jnp.exp2(b[c - 1:c, :] - b)
        intra.append(_dot(scores, v) + jnp.sum(q * k, axis=-1, keepdims=True) * v.astype(F32))
        qbs.append((q * wb).astype(BF16))
        upds.append(_dot_tn((k * we).astype(BF16), v))
        dcols.append(jnp.broadcast_to(wb[c - 1:c, :], (SUBLANES, HEAD_DIM)).T[:, :1])

    st = st_ref[...]
    states = []
    for ci in range(nc):
        states.append(st.astype(BF16))
        st = st * dcols[ci] + upds[ci]
    st_ref[...] = st

    for ci in range(nc):
        sl = pl.ds(ci * c, c)
        o = _dot(qbs[ci], states[ci]) + intra[ci]
        o_ref[sl, :] = (_rms(o, hg) * gt_ref[sl, :].astype(F32)).astype(BF16)


def _gla(q, k, v, logf, gate, head_gain, batch, seq, ts):
    m, d = q.shape
    heads = d // HEAD_DIM
    ns = seq // ts
    dmat, lev, big = _gla_tables(GLA_CHUNK)
    spec = pl.BlockSpec((ts, HEAD_DIM), lambda b, h, s: (b * ns + s, h))
    const = lambda a: pl.BlockSpec(a.shape, lambda b, h, s: (0,) * a.ndim)
    return pl.pallas_call(
        functools.partial(_gla_kernel, GLA_CHUNK),
        grid=(batch, heads, ns),
        in_specs=[spec, spec, spec, spec, spec,
                  pl.BlockSpec((1, HEAD_DIM), lambda b, h, s: (0, h)),
                  const(dmat), const(lev), const(big)],
        out_specs=spec,
        out_shape=jax.ShapeDtypeStruct((m, d), BF16),
        scratch_shapes=[pltpu.VMEM((HEAD_DIM, HEAD_DIM), F32)],
        compiler_params=_params("arbitrary", "arbitrary", "arbitrary"),
        name="gla",
    )(q, k, v, logf, gate, head_gain.reshape(1, d), jnp.asarray(dmat, BF16), jnp.asarray(lev),
      jnp.asarray(big))


def _proj_res_kernel(h_ref, a_ref, w_ref, o_ref, wb_ref):
    @pl.when(pl.program_id(0) == 0)
    def _():
        wb_ref[...] = w_ref[...].astype(BF16)

    o_ref[...] = h_ref[...] + _dot(a_ref[...], wb_ref[...])


def _resident_spec(w, layer):
    return pl.BlockSpec((None,) + w.shape[1:], lambda i: (layer, 0, 0), pipeline_mode=pl.Buffered(1))


def _proj_res(h, a, w, layer, tm):
    m, d = h.shape
    kdim = a.shape[1]
    return pl.pallas_call(
        _proj_res_kernel,
        grid=(m // tm,),
        in_specs=[pl.BlockSpec((tm, d), lambda i: (i, 0)),
                  pl.BlockSpec((tm, kdim), lambda i: (i, 0)),
                  _resident_spec(w, layer)],
        out_specs=pl.BlockSpec((tm, d), lambda i: (i, 0)),
        out_shape=jax.ShapeDtypeStruct((m, d), F32),
        scratch_shapes=[pltpu.VMEM((kdim, d), BF16)],
        compiler_params=_params("arbitrary"),
        name="proj_res",
    )(h, a, w)


def _mlp_kernel(h_ref, ng_ref, w1_ref, w2_ref, o_ref, xn_ref):
    @pl.when(pl.program_id(1) == 0)
    def _():
        x = h_ref[...]
        xn_ref[...] = _rms(x, ng_ref[...]).astype(BF16)
        o_ref[...] = x

    t = jnp.square(jnp.maximum(_wdot(xn_ref[...], w1_ref), 0.0)).astype(BF16)
    o_ref[...] += _wdot(t, w2_ref)


def _mlp(h, norm_g, w1, w2, layer, tm, tf):
    m, d = h.shape
    ff = w1.shape[-1]
    return pl.pallas_call(
        _mlp_kernel,
        grid=(m // tm, ff // tf),
        in_specs=[pl.BlockSpec((tm, d), lambda i, j: (i, 0)),
                  pl.BlockSpec((1, d), lambda i, j: (0, 0)),
                  _layer_spec(layer, (d, tf), lambda i, j: (0, j)),
                  _layer_spec(layer, (tf, d), lambda i, j: (j, 0))],
        out_specs=pl.BlockSpec((tm, d), lambda i, j: (i, 0)),
        out_shape=jax.ShapeDtypeStruct((m, d), F32),
        scratch_shapes=[pltpu.VMEM((tm, d), BF16)],
        compiler_params=_params("arbitrary", "arbitrary"),
        name="mlp",
    )(h, norm_g.reshape(1, d), w1, w2)


def _norm_proj_kernel(head_norm, block_mean, out_scale, *refs):
    h_ref, ng_ref, hn_ref, w_ref = refs[:4]
    outs = refs[4:-1]
    wb_ref = refs[-1]

    @pl.when(pl.program_id(0) == 0)
    def _():
        wb_ref[...] = w_ref[...].astype(BF16)

    tm = h_ref.shape[0]
    sub = MOBA_BLOCK if tm % MOBA_BLOCK == 0 else tm
    for r0 in range(0, tm, sub):
        rows = slice(r0, r0 + sub)
        y = _dot(_rms(h_ref[rows, :], ng_ref[...]).astype(BF16), wb_ref[...])
        if not head_norm:
            outs[0][rows, :] = y.astype(BF16)
            continue
        for hh, yh in enumerate(_head_rms(y, hn_ref[...])):
            cols = slice(hh * HEAD_DIM, (hh + 1) * HEAD_DIM)
            outs[0][rows, cols] = (yh if out_scale is None else yh * out_scale).astype(BF16)
            if block_mean:
                outs[1][0, r0 // MOBA_BLOCK:(r0 + sub) // MOBA_BLOCK, cols] = (
                    jnp.sum(yh.reshape(sub // MOBA_BLOCK, MOBA_BLOCK, HEAD_DIM), axis=1) * (1.0 / MOBA_BLOCK))


def _norm_proj(h, norm_g, head_g, w, wspec, tm, head_norm, block_mean, name, out_scale=None):
    m, d = h.shape
    n = wspec.block_shape[-1]
    nblk = tm // MOBA_BLOCK
    out_specs = [pl.BlockSpec((tm, n), lambda i: (i, 0))]
    out_shape = [jax.ShapeDtypeStruct((m, n), BF16)]
    if block_mean:
        out_specs.append(pl.BlockSpec((1, nblk, n), lambda i: (i, 0, 0)))
        out_shape.append(jax.ShapeDtypeStruct((m // tm, nblk, n), F32))
    return pl.pallas_call(
        functools.partial(_norm_proj_kernel, head_norm, block_mean, out_scale),
        grid=(m // tm,),
        in_specs=[pl.BlockSpec((tm, d), lambda i: (i, 0)),
                  pl.BlockSpec((1, d), lambda i: (0, 0)),
                  pl.BlockSpec((1, HEAD_DIM), lambda i: (0, 0)),
                  wspec],
        out_specs=out_specs,
        out_shape=out_shape,
        scratch_shapes=[pltpu.VMEM((d, n), BF16)],
        compiler_params=_params("arbitrary"),
        name=name,
    )(h, norm_g.reshape(1, d), head_g.reshape(1, HEAD_DIM), w)


def _moba_kernel(nblk, q_ref, k_ref, v_ref, km_ref, slope_ref, o_ref, vt_ref, bias_ref):
    blk = MOBA_BLOCK
    grp = blk // SUBLANES
    for n in range(nblk):
        vt_ref[0:HEAD_DIM, n * blk:(n + 1) * blk] = v_ref[n * blk:(n + 1) * blk, :].astype(F32).T.astype(BF16)
    vt_ref[HEAD_DIM:, :] = jnp.ones((vt_ref.shape[0] - HEAD_DIM, vt_ref.shape[1]), BF16)
    slope = slope_ref[0][:, :1]
    kmean = km_ref[0].astype(BF16)
    t_idx = lax.broadcasted_iota(jnp.int32, (blk, blk), 1)
    s_idx = lax.broadcasted_iota(jnp.int32, (blk, blk), 0)
    dist0 = (t_idx - s_idx).astype(F32)
    bias_ref[0] = jnp.where(dist0 >= 0.0, (LOG2E * slope) * dist0, -NEG_BIG)
    for dlt in range(1, nblk):
        bias_ref[dlt] = (LOG2E * slope) * (dist0 + float(dlt * blk))
    n_idx = lax.broadcasted_iota(jnp.int32, (nblk, blk), 0)

    def scores(i):
        qi = q_ref[i * blk:(i + 1) * blk, :]
        sel = None
        if i > MOBA_TOPK:
            gm = jnp.where(n_idx < i, _dot_nt(kmean, qi), NEG_BIG)
            rank = jnp.zeros((nblk, blk), F32)
            for mrow in range(nblk):
                gr = gm[mrow:mrow + 1, :]
                rank = rank + ((gr > gm) | ((gr == gm) & (mrow < n_idx))).astype(F32)
            sel = ((rank < MOBA_TOPK) & (n_idx < i)).astype(F32)
        return sel, _dot_nt(k_ref[0:(i + 1) * blk, :], qi)

    def softmax(i, sel, s_all):
        s = []
        for j in range(i + 1):
            sj = s_all[j * blk:(j + 1) * blk].reshape(grp, SUBLANES, blk) \
                - bias_ref[i - j].reshape(grp, SUBLANES, blk)
            if sel is not None and j < i:
                keep = jnp.broadcast_to(sel[j:j + 1, :], (SUBLANES, blk)) > 0.0
                sj = jnp.where(keep[None], sj, NEG_BIG)
            s.append(sj)
        m8 = functools.reduce(jnp.maximum, [jnp.max(sj, axis=0) for sj in s])
        m = jnp.broadcast_to(jnp.max(m8, axis=0, keepdims=True), (SUBLANES, blk))
        return jnp.concatenate([jnp.exp2(sj - m[None]).reshape(blk, blk).astype(BF16) for sj in s], axis=0)

    nxt = scores(0)
    for i in range(nblk):
        cur = nxt
        if i + 1 < nblk:
            nxt = scores(i + 1)
        acc = _dot(vt_ref[:, 0:(i + 1) * blk], softmax(i, *cur))
        o_ref[i * blk:(i + 1) * blk, :] = (acc[0:HEAD_DIM] / acc[HEAD_DIM:HEAD_DIM + 1]).T.astype(BF16)


def _moba(q, k, v, kmean, batch, seq):
    m, d = q.shape
    heads = d // HEAD_DIM
    nblk = seq // MOBA_BLOCK
    slopes = 2.0 ** (-8.0 * jnp.arange(1, heads + 1, dtype=F32) / heads)
    slopes = jnp.broadcast_to(slopes[:, None, None], (heads, 1, HEAD_DIM))
    spec = pl.BlockSpec((seq, HEAD_DIM), lambda b, h: (b, h))
    return pl.pallas_call(
        functools.partial(_moba_kernel, nblk),
        grid=(batch, heads),
        in_specs=[spec, spec, spec,
                  pl.BlockSpec((1, nblk, HEAD_DIM), lambda b, h: (b, 0, h)),
                  pl.BlockSpec((1, 1, HEAD_DIM), lambda b, h: (h, 0, 0))],
        out_specs=spec,
        out_shape=jax.ShapeDtypeStruct((m, d), BF16),
        scratch_shapes=[pltpu.VMEM((HEAD_DIM + 2 * SUBLANES, seq), BF16),
                        pltpu.VMEM((nblk, MOBA_BLOCK, MOBA_BLOCK), F32)],
        compiler_params=_params("arbitrary", "arbitrary"),
        name="moba_attn",
    )(q, k, v, kmean, slopes)


def _tile(n, pref):
    return pref if n % pref == 0 else n


def kernel(x, a_norm, a_w_in, a_head_norm, a_w_out, lower_bounds, kv_norm, w_kv, k_norm,
           b_norm, b_w_q, b_q_norm, b_w_o, mlp_norm, mlp_w1, mlp_w2):
    batch, seq, d = x.shape
    n_a = a_w_in.shape[0]
    n_b = b_w_q.shape[0]
    assert seq % MOBA_BLOCK == 0 and d % HEAD_DIM == 0
    m = batch * seq
    tm = _tile(m, 1024)
    ts = _tile(seq, 2048)
    tr = _tile(m, 512)

    h = x.reshape(m, d)
    kb = vb = kmean = None
    for l in range(n_a + n_b):
        if l < n_a:
            q, logf, k, v, gate = _hgrn_in(h, a_norm[l], lower_bounds, a_w_in, l, tm, 256)
            a = _gla(q, k, v, logf, gate, a_head_norm[l], batch, seq, ts)
            h = _proj_res(h, a, a_w_out, l, tr)
        else:
            if l == n_a:
                half = lambda c: pl.BlockSpec((d, d), lambda i: (0, c), pipeline_mode=pl.Buffered(1))
                kb, kmean = _norm_proj(h, kv_norm, k_norm, w_kv, half(0), tr, True, True, "shared_k")
                vb, = _norm_proj(h, kv_norm, k_norm, w_kv, half(1), tr, False, False, "shared_v")
                kmean = kmean.reshape(batch, seq // MOBA_BLOCK, d)
            j = l - n_a
            q, = _norm_proj(h, b_norm[j], b_q_norm[j], b_w_q, _resident_spec(b_w_q, j), tr, True, False,
                            "moba_q", out_scale=LOG2E * HEAD_DIM ** -0.5)
            a = _moba(q, kb, vb, kmean, batch, seq)
            h = _proj_res(h, a, b_w_o, j, tr)
        h = _mlp(h, mlp_norm[l], mlp_w1, mlp_w2, l, tm, 512)
    return h.reshape(batch, seq, d)
```

```python
import functools

import numpy as np
import jax
import jax.numpy as jnp
from jax import lax
from jax.experimental import pallas as pl
from jax.experimental.pallas import tpu as pltpu

F32 = jnp.float32
BF16 = jnp.bfloat16

HEAD_DIM = 128
MOBA_BLOCK = 256
MOBA_TOPK = 3
GLA_CHUNK = 64
EPS = 1e-6
NEG_BIG = -1e30
LB_FLOOR = 1e-30
LOG2E = 1.4426950408889634
SUBLANES = 8

V7X_VMEM_LIMIT_BYTES = 60 * 1024 * 1024


def _params(*sem):
    return pltpu.CompilerParams(dimension_semantics=sem, vmem_limit_bytes=V7X_VMEM_LIMIT_BYTES)


def _dot(a, b):
    return jnp.dot(a, b, preferred_element_type=F32)


def _wdot(a, w_ref):
    return jnp.dot(a, w_ref[...].astype(BF16), preferred_element_type=F32)


def _dot_nt(a, b):
    return lax.dot_general(a, b, (((1,), (1,)), ((), ())), preferred_element_type=F32)


def _dot_tn(a, b):
    return lax.dot_general(a, b, (((0,), (0,)), ((), ())), preferred_element_type=F32)


def _layer_spec(layer, block, index):
    return pl.BlockSpec((None,) + block, lambda i, j: (layer,) + index(i, j))


def _cast_specs(w, layer, steps, step_index):
    rows, cols = w.shape[1] // steps, w.shape[2]
    return (pl.BlockSpec((None, rows, cols), lambda *g: (layer, step_index(*g), 0)),
            pl.BlockSpec((rows, cols), lambda *g: (step_index(*g), 0)),
            jax.ShapeDtypeStruct(w.shape[1:], BF16))


def _rms(x, g):
    return x * lax.rsqrt(jnp.mean(x * x, axis=-1, keepdims=True) + EPS) * g


def _silu(x):
    return x * jax.nn.sigmoid(x)


def _head_rms(x, g):
    outs = []
    for hh in range(x.shape[1] // HEAD_DIM):
        outs.append(_rms(x[:, hh * HEAD_DIM:(hh + 1) * HEAD_DIM], g))
    return outs


def _hgrn_in_kernel(layer, h_ref, ng_ref, lb_ref, wq_ref, wf_ref, wi_ref, wg_ref,
                    q_ref, lf_ref, k_ref, v_ref, gt_ref, xn_ref):
    @pl.when(pl.program_id(1) == 0)
    def _():
        xn_ref[...] = _rms(h_ref[...], ng_ref[...]).astype(BF16)

    xn = xn_ref[...]
    pf = _wdot(xn, wf_ref)
    pq = _wdot(xn, wq_ref)
    pg = _wdot(xn, wg_ref)
    pv = _wdot(xn, wi_ref)

    lbr = lb_ref[...]
    rows = [lbr[r:r + 1, :] for r in range(lbr.shape[0])]
    mx = functools.reduce(jnp.maximum, rows)
    ex = [jnp.exp(r - mx) for r in rows]
    den = functools.reduce(lambda a, b: a + b, ex)
    p = [e / den for e in ex]
    lb = functools.reduce(lambda a, b: a + b, p[:layer + 1]) - p[0]

    e = jnp.exp(-jnp.abs(pf))
    r = 1.0 / (1.0 + e)
    er = e * r
    pos = pf >= 0.0
    one_m = 1.0 - lb
    lf_ref[...] = jnp.log(jnp.maximum(lb, LB_FLOOR) + one_m * jnp.where(pos, r, er))
    k_ref[...] = (one_m * jnp.where(pos, er, r)).astype(BF16)
    q_ref[...] = _silu(pq).astype(BF16)
    gt_ref[...] = _silu(pg).astype(BF16)
    v_ref[...] = pv.astype(BF16)


def _hgrn_in(h, norm_g, lower_bounds, w_in, layer, tm, tn):
    m, d = h.shape
    nj = d // tn
    wspec = lambda g: _layer_spec(layer, (d, tn), lambda i, j: (0, j + g * nj))
    ospec = pl.BlockSpec((tm, tn), lambda i, j: (i, j))
    return pl.pallas_call(
        functools.partial(_hgrn_in_kernel, layer),
        grid=(m // tm, nj),
        in_specs=[pl.BlockSpec((tm, d), lambda i, j: (i, 0)),
                  pl.BlockSpec((1, d), lambda i, j: (0, 0)),
                  pl.BlockSpec((lower_bounds.shape[0], tn), lambda i, j: (0, j)),
                  wspec(0), wspec(1), wspec(2), wspec(3)],
        out_specs=[ospec] * 5,
        out_shape=[jax.ShapeDtypeStruct((m, d), BF16), jax.ShapeDtypeStruct((m, d), F32),
                   jax.ShapeDtypeStruct((m, d), BF16), jax.ShapeDtypeStruct((m, d), BF16),
                   jax.ShapeDtypeStruct((m, d), BF16)],
        scratch_shapes=[pltpu.VMEM((tm, d), BF16)],
        compiler_params=_params("arbitrary", "arbitrary"),
        name="hgrn_in",
    )(h, norm_g.reshape(1, d), lower_bounds, w_in, w_in, w_in, w_in)


def _gla_tables(c):
    r = np.arange(c)
    j = r[None, :]
    mats = [j <= r[:, None]]
    small = (4, 2, 1)
    for m in small:
        seg, pos = r // (2 * m), r % (2 * m)
        mid = (seg * 2 * m + m)[:, None]
        second = (pos >= m)[:, None]
        mats.append(np.where(second, (j >= mid) & (j <= r[:, None]), (j > r[:, None]) & (j <= mid - 1)))
    d = np.concatenate(mats, 0).astype(np.float32)
    dmat = np.concatenate([d, d, d], axis=1)
    lev = np.full((c, c), -1, np.int32)
    for li, m in enumerate(small):
        seg, pos = r // (2 * m), r % (2 * m)
        ok = (seg[:, None] == seg[None, :]) & (pos >= m)[:, None] & (pos < m)[None, :]
        lev[ok] = li
    big = []
    m = c // 2
    while m >= SUBLANES:
        t = np.concatenate([np.arange(s0 + m, s0 + 2 * m) for s0 in range(0, c, 2 * m)])
        ok = (t[:, None] // (2 * m) == r[None, :] // (2 * m)) & ((r % (2 * m)) < m)[None, :]
        big.append(ok.astype(np.float32))
        m //= 2
    return dmat, lev, np.stack(big)


def _gla_kernel(chunk, q_ref, k_ref, v_ref, g_ref, gt_ref, hg_ref, dmat_ref, lev_ref, big_ref,
                w1_ref, w2_ref, o_ref, w1b_ref, w2b_ref, st_ref):
    @pl.when(pl.program_id(2) == 0)
    def _():
        st_ref[...] = jnp.zeros_like(st_ref)

    w1b_ref[...] = w1_ref[...].astype(BF16)
    w2b_ref[...] = w2_ref[...].astype(BF16)

    c = chunk
    nc = q_ref.shape[0] // c
    lev = lev_ref[...]
    hg = hg_ref[...]

    g = g_ref[...] * LOG2E
    g_hi = g.astype(BF16)
    r1 = g - g_hi.astype(F32)
    g_mid = r1.astype(BF16)
    g_lo = (r1 - g_mid.astype(F32)).astype(BF16)
    lanes = lambda x: jnp.concatenate([x[i * c:(i + 1) * c] for i in range(nc)], axis=1)
    e_all = _dot(dmat_ref[...], jnp.concatenate([lanes(g_hi), lanes(g_mid), lanes(g_lo)], axis=0))

    small_p, big_p, big_tgt = [], [], []
    for ci in range(nc):
        sl = pl.ds(ci * c, c)
        cols = slice(ci * HEAD_DIM, (ci + 1) * HEAD_DIM)
        b = e_all[0:c, cols]
        q = q_ref[sl, :].astype(F32)
        k = k_ref[sl, :].astype(F32)
        ps = []
        for li in range(3):
            w = jnp.exp2(e_all[(li + 1) * c:(li + 2) * c, cols])
            ps.append(_dot_nt((q * w).astype(BF16), (k * w).astype(BF16)))
        small_p.append(ps)
        ps, tg = [], []
        m = c // 2
        while m >= SUBLANES:
            qs, ks, tgt = [], [], []
            for s0 in range(0, c, 2 * m):
                ref = b[s0 + m - 1:s0 + m, :]
                qs.append(q[s0 + m:s0 + 2 * m] * jnp.exp2(b[s0 + m:s0 + 2 * m] - ref))
                ks.append(k[s0:s0 + m] * jnp.exp2(ref - b[s0:s0 + m]))
                ks.append(k[s0 + m:s0 + 2 * m])
                tgt.extend(range((s0 + m) // SUBLANES, (s0 + 2 * m) // SUBLANES))
            ps.append(_dot_nt(jnp.concatenate(qs, 0).astype(BF16), jnp.concatenate(ks, 0).astype(BF16)))
            tg.append(tgt)
            m //= 2
        big_p.append(ps)
        big_tgt.append(tg)

    intra, qbs, upds, dcols = [], [], [], []
    for ci in range(nc):
        sl = pl.ds(ci * c, c)
        cols = slice(ci * HEAD_DIM, (ci + 1) * HEAD_DIM)
        b = e_all[0:c, cols]
        q = q_ref[sl, :].astype(F32)
        k = k_ref[sl, :].astype(F32)
        v = v_ref[sl, :]
        rows = [jnp.zeros((SUBLANES, c), F32) for _ in range(c // SUBLANES)]
        for li, p in enumerate(small_p[ci]):
            p = jnp.where(lev == li, p, 0.0)
            rows = [rw + p[i * SUBLANES:(i + 1) * SUBLANES] for i, rw in enumerate(rows)]
        for li, (p, tgt) in enumerate(zip(big_p[ci], big_tgt[ci])):
            p = p * big_ref[li]
            for n, i in enumerate(tgt):
                rows[i] = rows[i] + p[n * SUBLANES:(n + 1) * SUBLANES]
        scores = jnp.concatenate(rows, 0).astype(BF16)
        wb = jnp.exp2(b)
        we = jnp.exp2(b[c - 1:c, :] - b)
        intra.append(_dot(scores, v) + jnp.sum(q * k, axis=-1, keepdims=True) * v.astype(F32))
        qbs.append((q * wb).astype(BF16))
        upds.append(_dot_tn((k * we).astype(BF16), v))
        dcols.append(jnp.broadcast_to(wb[c - 1:c, :], (SUBLANES, HEAD_DIM)).T[:, :1])

    st = st_ref[...]
    states = []
    for ci in range(nc):
        states.append(st.astype(BF16))
        st = st * dcols[ci] + upds[ci]
    st_ref[...] = st

    for ci in range(nc):
        sl = pl.ds(ci * c, c)
        o = _dot(qbs[ci], states[ci]) + intra[ci]
        o_ref[sl, :] = (_rms(o, hg) * gt_ref[sl, :].astype(F32)).astype(BF16)


def _gla(q, k, v, logf, gate, head_gain, mlp_w1, mlp_w2, layer, batch, seq, ts):
    m, d = q.shape
    heads = d // HEAD_DIM
    ns = seq // ts
    dmat, lev, big = _gla_tables(GLA_CHUNK)
    spec = pl.BlockSpec((ts, HEAD_DIM), lambda b, h, s: (b * ns + s, h))
    const = lambda a: pl.BlockSpec(a.shape, lambda b, h, s: (0,) * a.ndim)
    step = lambda b, h, s: (b * heads + h) * ns + s
    w1_in, w1_out, w1_shape = _cast_specs(mlp_w1, layer, batch * heads * ns, step)
    w2_in, w2_out, w2_shape = _cast_specs(mlp_w2, layer, batch * heads * ns, step)
    return pl.pallas_call(
        functools.partial(_gla_kernel, GLA_CHUNK),
        grid=(batch, heads, ns),
        in_specs=[spec, spec, spec, spec, spec,
                  pl.BlockSpec((1, HEAD_DIM), lambda b, h, s: (0, h)),
                  const(dmat), const(lev), const(big), w1_in, w2_in],
        out_specs=[spec, w1_out, w2_out],
        out_shape=[jax.ShapeDtypeStruct((m, d), BF16), w1_shape, w2_shape],
        scratch_shapes=[pltpu.VMEM((HEAD_DIM, HEAD_DIM), F32)],
        compiler_params=_params("arbitrary", "arbitrary", "arbitrary"),
        name="gla",
    )(q, k, v, logf, gate, head_gain.reshape(1, d), jnp.asarray(dmat, BF16), jnp.asarray(lev),
      jnp.asarray(big), mlp_w1, mlp_w2)


def _proj_res_kernel(h_ref, a_ref, w_ref, o_ref, wb_ref):
    @pl.when(pl.program_id(0) == 0)
    def _():
        wb_ref[...] = w_ref[...].astype(BF16)

    o_ref[...] = h_ref[...] + _dot(a_ref[...], wb_ref[...])


def _resident_spec(w, layer):
    return pl.BlockSpec((None,) + w.shape[1:], lambda i: (layer, 0, 0), pipeline_mode=pl.Buffered(1))


def _proj_res(h, a, w, layer, tm):
    m, d = h.shape
    kdim = a.shape[1]
    return pl.pallas_call(
        _proj_res_kernel,
        grid=(m // tm,),
        in_specs=[pl.BlockSpec((tm, d), lambda i: (i, 0)),
                  pl.BlockSpec((tm, kdim), lambda i: (i, 0)),
                  _resident_spec(w, layer)],
        out_specs=pl.BlockSpec((tm, d), lambda i: (i, 0)),
        out_shape=jax.ShapeDtypeStruct((m, d), F32),
        scratch_shapes=[pltpu.VMEM((kdim, d), BF16)],
        compiler_params=_params("arbitrary"),
        name="proj_res",
    )(h, a, w)


def _mlp_kernel(h_ref, ng_ref, w1_ref, w2_ref, o_ref, xn_ref):
    @pl.when(pl.program_id(1) == 0)
    def _():
        x = h_ref[...]
        xn_ref[...] = _rms(x, ng_ref[...]).astype(BF16)
        o_ref[...] = x

    t = jnp.square(jnp.maximum(_dot(xn_ref[...], w1_ref[...]), 0.0)).astype(BF16)
    o_ref[...] += _dot(t, w2_ref[...])


def _mlp(h, norm_g, w1, w2, tm, tf):
    m, d = h.shape
    ff = w1.shape[-1]
    return pl.pallas_call(
        _mlp_kernel,
        grid=(m // tm, ff // tf),
        in_specs=[pl.BlockSpec((tm, d), lambda i, j: (i, 0)),
                  pl.BlockSpec((1, d), lambda i, j: (0, 0)),
                  pl.BlockSpec((d, tf), lambda i, j: (0, j)),
                  pl.BlockSpec((tf, d), lambda i, j: (j, 0))],
        out_specs=pl.BlockSpec((tm, d), lambda i, j: (i, 0)),
        out_shape=jax.ShapeDtypeStruct((m, d), F32),
        scratch_shapes=[pltpu.VMEM((tm, d), BF16)],
        compiler_params=_params("arbitrary", "arbitrary"),
        name="mlp",
    )(h, norm_g.reshape(1, d), w1, w2)


def _norm_proj_kernel(head_norm, block_mean, out_scale, *refs):
    h_ref, ng_ref, hn_ref, w_ref = refs[:4]
    outs = refs[4:-1]
    wb_ref = refs[-1]

    @pl.when(pl.program_id(0) == 0)
    def _():
        wb_ref[...] = w_ref[...].astype(BF16)

    tm = h_ref.shape[0]
    sub = MOBA_BLOCK if tm % MOBA_BLOCK == 0 else tm
    for r0 in range(0, tm, sub):
        rows = slice(r0, r0 + sub)
        y = _dot(_rms(h_ref[rows, :], ng_ref[...]).astype(BF16), wb_ref[...])
        if not head_norm:
            outs[0][rows, :] = y.astype(BF16)
            continue
        for hh, yh in enumerate(_head_rms(y, hn_ref[...])):
            cols = slice(hh * HEAD_DIM, (hh + 1) * HEAD_DIM)
            outs[0][rows, cols] = (yh if out_scale is None else yh * out_scale).astype(BF16)
            if block_mean:
                outs[1][0, r0 // MOBA_BLOCK:(r0 + sub) // MOBA_BLOCK, cols] = (
                    jnp.sum(yh.reshape(sub // MOBA_BLOCK, MOBA_BLOCK, HEAD_DIM), axis=1) * (1.0 / MOBA_BLOCK))


def _norm_proj(h, norm_g, head_g, w, wspec, tm, head_norm, block_mean, name, out_scale=None):
    m, d = h.shape
    n = wspec.block_shape[-1]
    nblk = tm // MOBA_BLOCK
    out_specs = [pl.BlockSpec((tm, n), lambda i: (i, 0))]
    out_shape = [jax.ShapeDtypeStruct((m, n), BF16)]
    if block_mean:
        out_specs.append(pl.BlockSpec((1, nblk, n), lambda i: (i, 0, 0)))
        out_shape.append(jax.ShapeDtypeStruct((m // tm, nblk, n), F32))
    return pl.pallas_call(
        functools.partial(_norm_proj_kernel, head_norm, block_mean, out_scale),
        grid=(m // tm,),
        in_specs=[pl.BlockSpec((tm, d), lambda i: (i, 0)),
                  pl.BlockSpec((1, d), lambda i: (0, 0)),
                  pl.BlockSpec((1, HEAD_DIM), lambda i: (0, 0)),
                  wspec],
        out_specs=out_specs,
        out_shape=out_shape,
        scratch_shapes=[pltpu.VMEM((d, n), BF16)],
        compiler_params=_params("arbitrary"),
        name=name,
    )(h, norm_g.reshape(1, d), head_g.reshape(1, HEAD_DIM), w)


def _moba_kernel(nblk, q_ref, k_ref, v_ref, km_ref, slope_ref, w1_ref, w2_ref,
                 o_ref, w1b_ref, w2b_ref, vt_ref, bias_ref):
    w1b_ref[...] = w1_ref[...].astype(BF16)
    w2b_ref[...] = w2_ref[...].astype(BF16)

    blk = MOBA_BLOCK
    grp = blk // SUBLANES
    for n in range(nblk):
        vt_ref[0:HEAD_DIM, n * blk:(n + 1) * blk] = v_ref[n * blk:(n + 1) * blk, :].astype(F32).T.astype(BF16)
    vt_ref[HEAD_DIM:, :] = jnp.ones((vt_ref.shape[0] - HEAD_DIM, vt_ref.shape[1]), BF16)
    slope = slope_ref[0][:, :1]
    kmean = km_ref[0].astype(BF16)
    t_idx = lax.broadcasted_iota(jnp.int32, (blk, blk), 1)
    s_idx = lax.broadcasted_iota(jnp.int32, (blk, blk), 0)
    dist0 = (t_idx - s_idx).astype(F32)
    bias_ref[0] = jnp.where(dist0 >= 0.0, (LOG2E * slope) * dist0, -NEG_BIG)
    for dlt in range(1, nblk):
        bias_ref[dlt] = (LOG2E * slope) * (dist0 + float(dlt * blk))
    n_idx = lax.broadcasted_iota(jnp.int32, (nblk, blk), 0)

    def scores(i):
        qi = q_ref[i * blk:(i + 1) * blk, :]
        sel = None
        if i > MOBA_TOPK:
            gm = jnp.where(n_idx < i, _dot_nt(kmean, qi), NEG_BIG)
            rank = jnp.zeros((nblk, blk), F32)
            for mrow in range(nblk):
                gr = gm[mrow:mrow + 1, :]
                rank = rank + ((gr > gm) | ((gr == gm) & (mrow < n_idx))).astype(F32)
            sel = ((rank < MOBA_TOPK) & (n_idx < i)).astype(F32)
        return sel, _dot_nt(k_ref[0:(i + 1) * blk, :], qi)

    def softmax(i, sel, s_all):
        s = []
        for j in range(i + 1):
            sj = s_all[j * blk:(j + 1) * blk].reshape(grp, SUBLANES, blk) \
                - bias_ref[i - j].reshape(grp, SUBLANES, blk)
            if sel is not None and j < i:
                keep = jnp.broadcast_to(sel[j:j + 1, :], (SUBLANES, blk)) > 0.0
                sj = jnp.where(keep[None], sj, NEG_BIG)
            s.append(sj)
        m8 = functools.reduce(jnp.maximum, [jnp.max(sj, axis=0) for sj in s])
        m = jnp.broadcast_to(jnp.max(m8, axis=0, keepdims=True), (SUBLANES, blk))
        return jnp.concatenate([jnp.exp2(sj - m[None]).reshape(blk, blk).astype(BF16) for sj in s], axis=0)

    nxt = scores(0)
    for i in range(nblk):
        cur = nxt
        if i + 1 < nblk:
            nxt = scores(i + 1)
        acc = _dot(vt_ref[:, 0:(i + 1) * blk], softmax(i, *cur))
        o_ref[i * blk:(i + 1) * blk, :] = (acc[0:HEAD_DIM] / acc[HEAD_DIM:HEAD_DIM + 1]).T.astype(BF16)


def _moba(q, k, v, kmean, mlp_w1, mlp_w2, layer, batch, seq):
    m, d = q.shape
    heads = d // HEAD_DIM
    nblk = seq // MOBA_BLOCK
    step = lambda b, h: b * heads + h
    w1_in, w1_out, w1_shape = _cast_specs(mlp_w1, layer, batch * heads, step)
    w2_in, w2_out, w2_shape = _cast_specs(mlp_w2, layer, batch * heads, step)
    slopes = 2.0 ** (-8.0 * jnp.arange(1, heads + 1, dtype=F32) / heads)
    slopes = jnp.broadcast_to(slopes[:, None, None], (heads, 1, HEAD_DIM))
    spec = pl.BlockSpec((seq, HEAD_DIM), lambda b, h: (b, h))
    return pl.pallas_call(
        functools.partial(_moba_kernel, nblk),
        grid=(batch, heads),
        in_specs=[spec, spec, spec,
                  pl.BlockSpec((1, nblk, HEAD_DIM), lambda b, h: (b, 0, h)),
                  pl.BlockSpec((1, 1, HEAD_DIM), lambda b, h: (h, 0, 0)), w1_in, w2_in],
        out_specs=[spec, w1_out, w2_out],
        out_shape=[jax.ShapeDtypeStruct((m, d), BF16), w1_shape, w2_shape],
        scratch_shapes=[pltpu.VMEM((HEAD_DIM + 2 * SUBLANES, seq), BF16),
                        pltpu.VMEM((nblk, MOBA_BLOCK, MOBA_BLOCK), F32)],
        compiler_params=_params("arbitrary", "arbitrary"),
        name="moba_attn",
    )(q, k, v, kmean, slopes, mlp_w1, mlp_w2)


def _tile(n, pref):
    return pref if n % pref == 0 else n


def kernel(x, a_norm, a_w_in, a_head_norm, a_w_out, lower_bounds, kv_norm, w_kv, k_norm,
           b_norm, b_w_q, b_q_norm, b_w_o, mlp_norm, mlp_w1, mlp_w2):
    batch, seq, d = x.shape
    n_a = a_w_in.shape[0]
    n_b = b_w_q.shape[0]
    assert seq % MOBA_BLOCK == 0 and d % HEAD_DIM == 0
    m = batch * seq
    tm = _tile(m, 1024)
    ts = _tile(seq, 2048)
    tr = _tile(m, 512)

    h = x.reshape(m, d)
    kb = vb = kmean = None
    for l in range(n_a + n_b):
        if l < n_a:
            q, logf, k, v, gate = _hgrn_in(h, a_norm[l], lower_bounds, a_w_in, l, tm, 256)
            a, w1b, w2b = _gla(q, k, v, logf, gate, a_head_norm[l], mlp_w1, mlp_w2, l, batch, seq, ts)
            h = _proj_res(h, a, a_w_out, l, tr)
        else:
            if l == n_a:
                half = lambda c: pl.BlockSpec((d, d), lambda i: (0, c), pipeline_mode=pl.Buffered(1))
                kb, kmean = _norm_proj(h, kv_norm, k_norm, w_kv, half(0), tr, True, True, "shared_k")
                vb, = _norm_proj(h, kv_norm, k_norm, w_kv, half(1), tr, False, False, "shared_v")
                kmean = kmean.reshape(batch, seq // MOBA_BLOCK, d)
            j = l - n_a
            q, = _norm_proj(h, b_norm[j], b_q_norm[j], b_w_q, _resident_spec(b_w_q, j), tr, True, False,
                            "moba_q", out_scale=LOG2E * HEAD_DIM ** -0.5)
            a, w1b, w2b = _moba(q, kb, vb, kmean, mlp_w1, mlp_w2, l, batch, seq)
            h = _proj_res(h, a, b_w_o, j, tr)
        h = _mlp(h, mlp_norm[l], w1b, w2b, tm, 1024)
    return h.reshape(batch, seq, d)
```

```python
import functools

import numpy as np
import jax
import jax.numpy as jnp
from jax import lax
from jax.experimental import pallas as pl
from jax.experimental.pallas import tpu as pltpu

F32 = jnp.float32
BF16 = jnp.bfloat16

HEAD_DIM = 128
MOBA_BLOCK = 256
MOBA_TOPK = 3
GLA_CHUNK = 64
EPS = 1e-6
NEG_BIG = -1e30
LB_FLOOR = 1e-30
LOG2E = 1.4426950408889634
SUBLANES = 8

V7X_VMEM_LIMIT_BYTES = 60 * 1024 * 1024


def _params(*sem):
    return pltpu.CompilerParams(dimension_semantics=sem, vmem_limit_bytes=V7X_VMEM_LIMIT_BYTES)


def _dot(a, b):
    return jnp.dot(a, b, preferred_element_type=F32)


def _wdot(a, w_ref):
    return jnp.dot(a, w_ref[...].astype(BF16), preferred_element_type=F32)


def _dot_nt(a, b):
    return lax.dot_general(a, b, (((1,), (1,)), ((), ())), preferred_element_type=F32)


def _dot_tn(a, b):
    return lax.dot_general(a, b, (((0,), (0,)), ((), ())), preferred_element_type=F32)


def _layer_spec(layer, block, index):
    return pl.BlockSpec((None,) + block, lambda i, j: (layer,) + index(i, j))


def _cast_specs(w, layer, steps, step_index):
    rows, cols = w.shape[1] // steps, w.shape[2]
    return (pl.BlockSpec((None, rows, cols), lambda *g: (layer, step_index(*g), 0)),
            pl.BlockSpec((rows, cols), lambda *g: (step_index(*g), 0)),
            jax.ShapeDtypeStruct(w.shape[1:], BF16))


def _store_heads(ref, x):
    for hh in range(ref.shape[0]):
        ref[hh] = x[:, hh * HEAD_DIM:(hh + 1) * HEAD_DIM]


def _load_heads(ref):
    return jnp.concatenate([ref[hh] for hh in range(ref.shape[0])], axis=1)


def _rms(x, g):
    return x * lax.rsqrt(jnp.mean(x * x, axis=-1, keepdims=True) + EPS) * g


def _silu(x):
    return x * jax.nn.sigmoid(x)


def _head_rms(x, g):
    outs = []
    for hh in range(x.shape[1] // HEAD_DIM):
        outs.append(_rms(x[:, hh * HEAD_DIM:(hh + 1) * HEAD_DIM], g))
    return outs


def _hgrn_in_kernel(layer, h_ref, ng_ref, lb_ref, wq_ref, wf_ref, wi_ref, wg_ref,
                    q_ref, lf_ref, k_ref, v_ref, gt_ref, xn_ref):
    @pl.when(pl.program_id(1) == 0)
    def _():
        xn_ref[...] = _rms(h_ref[...], ng_ref[...]).astype(BF16)

    xn = xn_ref[...]
    pf = _wdot(xn, wf_ref)
    pq = _wdot(xn, wq_ref)
    pg = _wdot(xn, wg_ref)
    pv = _wdot(xn, wi_ref)

    lbr = lb_ref[...]
    rows = [lbr[r:r + 1, :] for r in range(lbr.shape[0])]
    mx = functools.reduce(jnp.maximum, rows)
    ex = [jnp.exp(r - mx) for r in rows]
    den = functools.reduce(lambda a, b: a + b, ex)
    p = [e / den for e in ex]
    lb = functools.reduce(lambda a, b: a + b, p[:layer + 1]) - p[0]

    e = jnp.exp(-jnp.abs(pf))
    r = 1.0 / (1.0 + e)
    er = e * r
    pos = pf >= 0.0
    one_m = 1.0 - lb
    _store_heads(lf_ref, jnp.log(jnp.maximum(lb, LB_FLOOR) + one_m * jnp.where(pos, r, er)))
    _store_heads(k_ref, (one_m * jnp.where(pos, er, r)).astype(BF16))
    _store_heads(q_ref, _silu(pq).astype(BF16))
    _store_heads(gt_ref, _silu(pg).astype(BF16))
    _store_heads(v_ref, pv.astype(BF16))


def _hgrn_in(h, norm_g, lower_bounds, w_in, layer, tm, tn):
    m, d = h.shape
    nj = d // tn
    wspec = lambda g: _layer_spec(layer, (d, tn), lambda i, j: (0, j + g * nj))
    heads = d // HEAD_DIM
    ospec = pl.BlockSpec((tn // HEAD_DIM, tm, HEAD_DIM), lambda i, j: (j, i, 0))
    return pl.pallas_call(
        functools.partial(_hgrn_in_kernel, layer),
        grid=(m // tm, nj),
        in_specs=[pl.BlockSpec((tm, d), lambda i, j: (i, 0)),
                  pl.BlockSpec((1, d), lambda i, j: (0, 0)),
                  pl.BlockSpec((lower_bounds.shape[0], tn), lambda i, j: (0, j)),
                  wspec(0), wspec(1), wspec(2), wspec(3)],
        out_specs=[ospec] * 5,
        out_shape=[jax.ShapeDtypeStruct((heads, m, HEAD_DIM), dt) for dt in (BF16, F32, BF16, BF16, BF16)],
        scratch_shapes=[pltpu.VMEM((tm, d), BF16)],
        compiler_params=_params("arbitrary", "arbitrary"),
        name="hgrn_in",
    )(h, norm_g.reshape(1, d), lower_bounds, w_in, w_in, w_in, w_in)


def _gla_tables(c):
    r = np.arange(c)
    j = r[None, :]
    mats = [j <= r[:, None]]
    small = (4, 2, 1)
    for m in small:
        seg, pos = r // (2 * m), r % (2 * m)
        mid = (seg * 2 * m + m)[:, None]
        second = (pos >= m)[:, None]
        mats.append(np.where(second, (j >= mid) & (j <= r[:, None]), (j > r[:, None]) & (j <= mid - 1)))
    d = np.concatenate(mats, 0).astype(np.float32)
    dmat = np.concatenate([d, d, d], axis=1)
    lev = np.full((c, c), -1, np.int32)
    for li, m in enumerate(small):
        seg, pos = r // (2 * m), r % (2 * m)
        ok = (seg[:, None] == seg[None, :]) & (pos >= m)[:, None] & (pos < m)[None, :]
        lev[ok] = li
    big = []
    m = c // 2
    while m >= SUBLANES:
        t = np.concatenate([np.arange(s0 + m, s0 + 2 * m) for s0 in range(0, c, 2 * m)])
        ok = (t[:, None] // (2 * m) == r[None, :] // (2 * m)) & ((r % (2 * m)) < m)[None, :]
        big.append(ok.astype(np.float32))
        m //= 2
    return dmat, lev, np.stack(big)


def _gla_kernel(chunk, q_ref, k_ref, v_ref, g_ref, gt_ref, hg_ref, dmat_ref, lev_ref, big_ref,
                w1_ref, w2_ref, o_ref, w1b_ref, w2b_ref, st_ref):
    @pl.when(pl.program_id(2) == 0)
    def _():
        st_ref[...] = jnp.zeros_like(st_ref)

    w1b_ref[...] = w1_ref[...].astype(BF16)
    w2b_ref[...] = w2_ref[...].astype(BF16)

    c = chunk
    nc = q_ref.shape[0] // c
    lev = lev_ref[...]
    hg = hg_ref[...]

    g = g_ref[...] * LOG2E
    g_hi = g.astype(BF16)
    r1 = g - g_hi.astype(F32)
    g_mid = r1.astype(BF16)
    g_lo = (r1 - g_mid.astype(F32)).astype(BF16)
    lanes = lambda x: jnp.concatenate([x[i * c:(i + 1) * c] for i in range(nc)], axis=1)
    e_all = _dot(dmat_ref[...], jnp.concatenate([lanes(g_hi), lanes(g_mid), lanes(g_lo)], axis=0))

    small_p, big_p, big_tgt = [], [], []
    for ci in range(nc):
        sl = pl.ds(ci * c, c)
        cols = slice(ci * HEAD_DIM, (ci + 1) * HEAD_DIM)
        b = e_all[0:c, cols]
        q = q_ref[sl, :].astype(F32)
        k = k_ref[sl, :].astype(F32)
        ps = []
        for li in range(3):
            w = jnp.exp2(e_all[(li + 1) * c:(li + 2) * c, cols])
            ps.append(_dot_nt((q * w).astype(BF16), (k * w).astype(BF16)))
        small_p.append(ps)
        ps, tg = [], []
        m = c // 2
        while m >= SUBLANES:
            qs, ks, tgt = [], [], []
            for s0 in range(0, c, 2 * m):
                ref = b[s0 + m - 1:s0 + m, :]
                qs.append(q[s0 + m:s0 + 2 * m] * jnp.exp2(b[s0 + m:s0 + 2 * m] - ref))
                ks.append(k[s0:s0 + m] * jnp.exp2(ref - b[s0:s0 + m]))
                ks.append(k[s0 + m:s0 + 2 * m])
                tgt.extend(range((s0 + m) // SUBLANES, (s0 + 2 * m) // SUBLANES))
            ps.append(_dot_nt(jnp.concatenate(qs, 0).astype(BF16), jnp.concatenate(ks, 0).astype(BF16)))
            tg.append(tgt)
            m //= 2
        big_p.append(ps)
        big_tgt.append(tg)

    intra, qbs, upds, dcols = [], [], [], []
    for ci in range(nc):
        sl = pl.ds(ci * c, c)
        cols = slice(ci * HEAD_DIM, (ci + 1) * HEAD_DIM)
        b = e_all[0:c, cols]
        q = q_ref[sl, :].astype(F32)
        k = k_ref[sl, :].astype(F32)
        v = v_ref[sl, :]
        rows = [jnp.zeros((SUBLANES, c), F32) for _ in range(c // SUBLANES)]
        for li, p in enumerate(small_p[ci]):
            p = jnp.where(lev == li, p, 0.0)
            rows = [rw + p[i * SUBLANES:(i + 1) * SUBLANES] for i, rw in enumerate(rows)]
        for li, (p, tgt) in enumerate(zip(big_p[ci], big_tgt[ci])):
            p = p * big_ref[li]
            for n, i in enumerate(tgt):
                rows[i] = rows[i] + p[n * SUBLANES:(n + 1) * SUBLANES]
        scores = jnp.concatenate(rows, 0).astype(BF16)
        wb = jnp.exp2(b)
        we = jnp.exp2(b[c - 1:c, :] - b)
        intra.append(_dot(scores, v) + jnp.sum(q * k, axis=-1, keepdims=True) * v.astype(F32))
        qbs.append((q * wb).astype(BF16))
        upds.append(_dot_tn((k * we).astype(BF16), v))
        dcols.append(jnp.broadcast_to(wb[c - 1:c, :], (SUBLANES, HEAD_DIM)).T[:, :1])

    st = st_ref[...]
    states = []
    for ci in range(nc):
        states.append(st.astype(BF16))
        st = st * dcols[ci] + upds[ci]
    st_ref[...] = st

    for ci in range(nc):
        sl = pl.ds(ci * c, c)
        o = _dot(qbs[ci], states[ci]) + intra[ci]
        o_ref[sl, :] = (_rms(o, hg) * gt_ref[sl, :].astype(F32)).astype(BF16)


def _gla(q, k, v, logf, gate, head_gain, mlp_w1, mlp_w2, layer, batch, seq, ts):
    heads, m, _ = q.shape
    d = heads * HEAD_DIM
    ns = seq // ts
    dmat, lev, big = _gla_tables(GLA_CHUNK)
    spec = pl.BlockSpec((None, ts, HEAD_DIM), lambda b, h, s: (h, b * ns + s, 0))
    const = lambda a: pl.BlockSpec(a.shape, lambda b, h, s: (0,) * a.ndim)
    step = lambda b, h, s: (b * heads + h) * ns + s
    w1_in, w1_out, w1_shape = _cast_specs(mlp_w1, layer, batch * heads * ns, step)
    w2_in, w2_out, w2_shape = _cast_specs(mlp_w2, layer, batch * heads * ns, step)
    return pl.pallas_call(
        functools.partial(_gla_kernel, GLA_CHUNK),
        grid=(batch, heads, ns),
        in_specs=[spec, spec, spec, spec, spec,
                  pl.BlockSpec((1, HEAD_DIM), lambda b, h, s: (0, h)),
                  const(dmat), const(lev), const(big), w1_in, w2_in],
        out_specs=[spec, w1_out, w2_out],
        out_shape=[jax.ShapeDtypeStruct((heads, m, HEAD_DIM), BF16), w1_shape, w2_shape],
        scratch_shapes=[pltpu.VMEM((HEAD_DIM, HEAD_DIM), F32)],
        compiler_params=_params("arbitrary", "arbitrary", "arbitrary"),
        name="gla",
    )(q, k, v, logf, gate, head_gain.reshape(1, d), jnp.asarray(dmat, BF16), jnp.asarray(lev),
      jnp.asarray(big), mlp_w1, mlp_w2)


def _proj_res_kernel(h_ref, a_ref, w_ref, o_ref, wb_ref):
    @pl.when(pl.program_id(0) == 0)
    def _():
        wb_ref[...] = w_ref[...].astype(BF16)

    o_ref[...] = h_ref[...] + _dot(_load_heads(a_ref), wb_ref[...])


def _resident_spec(w, layer):
    return pl.BlockSpec((None,) + w.shape[1:], lambda i: (layer, 0, 0), pipeline_mode=pl.Buffered(1))


def _proj_res(h, a, w, layer, tm):
    m, d = h.shape
    kdim = a.shape[0] * HEAD_DIM
    return pl.pallas_call(
        _proj_res_kernel,
        grid=(m // tm,),
        in_specs=[pl.BlockSpec((tm, d), lambda i: (i, 0)),
                  pl.BlockSpec((a.shape[0], tm, HEAD_DIM), lambda i: (0, i, 0)),
                  _resident_spec(w, layer)],
        out_specs=pl.BlockSpec((tm, d), lambda i: (i, 0)),
        out_shape=jax.ShapeDtypeStruct((m, d), F32),
        scratch_shapes=[pltpu.VMEM((kdim, d), BF16)],
        compiler_params=_params("arbitrary"),
        name="proj_res",
    )(h, a, w)


def _mlp_kernel(h_ref, ng_ref, w1_ref, w2_ref, o_ref, xn_ref):
    @pl.when(pl.program_id(1) == 0)
    def _():
        x = h_ref[...]
        xn_ref[...] = _rms(x, ng_ref[...]).astype(BF16)
        o_ref[...] = x

    t = jnp.square(jnp.maximum(_dot(xn_ref[...], w1_ref[...]), 0.0)).astype(BF16)
    o_ref[...] += _dot(t, w2_ref[...])


def _mlp(h, norm_g, w1, w2, tm, tf):
    m, d = h.shape
    ff = w1.shape[-1]
    return pl.pallas_call(
        _mlp_kernel,
        grid=(m // tm, ff // tf),
        in_specs=[pl.BlockSpec((tm, d), lambda i, j: (i, 0)),
                  pl.BlockSpec((1, d), lambda i, j: (0, 0)),
                  pl.BlockSpec((d, tf), lambda i, j: (0, j)),
                  pl.BlockSpec((tf, d), lambda i, j: (j, 0))],
        out_specs=pl.BlockSpec((tm, d), lambda i, j: (i, 0)),
        out_shape=jax.ShapeDtypeStruct((m, d), F32),
        scratch_shapes=[pltpu.VMEM((tm, d), BF16)],
        compiler_params=_params("arbitrary", "arbitrary"),
        name="mlp",
    )(h, norm_g.reshape(1, d), w1, w2)


def _norm_proj_kernel(head_norm, block_mean, out_scale, *refs):
    h_ref, ng_ref, hn_ref, w_ref = refs[:4]
    outs = refs[4:-1]
    wb_ref = refs[-1]

    @pl.when(pl.program_id(0) == 0)
    def _():
        wb_ref[...] = w_ref[...].astype(BF16)

    tm = h_ref.shape[0]
    sub = MOBA_BLOCK if tm % MOBA_BLOCK == 0 else tm
    for r0 in range(0, tm, sub):
        rows = slice(r0, r0 + sub)
        y = _dot(_rms(h_ref[rows, :], ng_ref[...]).astype(BF16), wb_ref[...])
        if not head_norm:
            for hh in range(outs[0].shape[0]):
                outs[0][hh, rows, :] = y[:, hh * HEAD_DIM:(hh + 1) * HEAD_DIM].astype(BF16)
            continue
        for hh, yh in enumerate(_head_rms(y, hn_ref[...])):
            cols = slice(hh * HEAD_DIM, (hh + 1) * HEAD_DIM)
            outs[0][hh, rows, :] = (yh if out_scale is None else yh * out_scale).astype(BF16)
            if block_mean:
                outs[1][0, r0 // MOBA_BLOCK:(r0 + sub) // MOBA_BLOCK, cols] = (
                    jnp.sum(yh.reshape(sub // MOBA_BLOCK, MOBA_BLOCK, HEAD_DIM), axis=1) * (1.0 / MOBA_BLOCK))


def _norm_proj(h, norm_g, head_g, w, wspec, tm, head_norm, block_mean, name, out_scale=None):
    m, d = h.shape
    n = wspec.block_shape[-1]
    nblk = tm // MOBA_BLOCK
    out_specs = [pl.BlockSpec((n // HEAD_DIM, tm, HEAD_DIM), lambda i: (0, i, 0))]
    out_shape = [jax.ShapeDtypeStruct((n // HEAD_DIM, m, HEAD_DIM), BF16)]
    if block_mean:
        out_specs.append(pl.BlockSpec((1, nblk, n), lambda i: (i, 0, 0)))
        out_shape.append(jax.ShapeDtypeStruct((m // tm, nblk, n), F32))
    return pl.pallas_call(
        functools.partial(_norm_proj_kernel, head_norm, block_mean, out_scale),
        grid=(m // tm,),
        in_specs=[pl.BlockSpec((tm, d), lambda i: (i, 0)),
                  pl.BlockSpec((1, d), lambda i: (0, 0)),
                  pl.BlockSpec((1, HEAD_DIM), lambda i: (0, 0)),
                  wspec],
        out_specs=out_specs,
        out_shape=out_shape,
        scratch_shapes=[pltpu.VMEM((d, n), BF16)],
        compiler_params=_params("arbitrary"),
        name=name,
    )(h, norm_g.reshape(1, d), head_g.reshape(1, HEAD_DIM), w)


def _moba_kernel(nblk, q_ref, k_ref, v_ref, km_ref, slope_ref, w1_ref, w2_ref,
                 o_ref, w1b_ref, w2b_ref, vt_ref, bias_ref):
    w1b_ref[...] = w1_ref[...].astype(BF16)
    w2b_ref[...] = w2_ref[...].astype(BF16)

    blk = MOBA_BLOCK
    grp = blk // SUBLANES
    for n in range(nblk):
        vt_ref[0:HEAD_DIM, n * blk:(n + 1) * blk] = v_ref[n * blk:(n + 1) * blk, :].astype(F32).T.astype(BF16)
    vt_ref[HEAD_DIM:, :] = jnp.ones((vt_ref.shape[0] - HEAD_DIM, vt_ref.shape[1]), BF16)
    slope = slope_ref[0][:, :1]
    kmean = km_ref[0].astype(BF16)
    t_idx = lax.broadcasted_iota(jnp.int32, (blk, blk), 1)
    s_idx = lax.broadcasted_iota(jnp.int32, (blk, blk), 0)
    dist0 = (t_idx - s_idx).astype(F32)
    bias_ref[0] = jnp.where(dist0 >= 0.0, (LOG2E * slope) * dist0, -NEG_BIG)
    for dlt in range(1, nblk):
        bias_ref[dlt] = (LOG2E * slope) * (dist0 + float(dlt * blk))
    n_idx = lax.broadcasted_iota(jnp.int32, (nblk, blk), 0)

    def scores(i):
        qi = q_ref[i * blk:(i + 1) * blk, :]
        sel = None
        if i > MOBA_TOPK:
            gm = jnp.where(n_idx < i, _dot_nt(kmean, qi), NEG_BIG)
            rank = jnp.zeros((nblk, blk), F32)
            for mrow in range(nblk):
                gr = gm[mrow:mrow + 1, :]
                rank = rank + ((gr > gm) | ((gr == gm) & (mrow < n_idx))).astype(F32)
            sel = ((rank < MOBA_TOPK) & (n_idx < i)).astype(F32)
        return sel, _dot_nt(k_ref[0:(i + 1) * blk, :], qi)

    def softmax(i, sel, s_all):
        s = []
        for j in range(i + 1):
            sj = s_all[j * blk:(j + 1) * blk].reshape(grp, SUBLANES, blk) \
                - bias_ref[i - j].reshape(grp, SUBLANES, blk)
            if sel is not None and j < i:
                keep = jnp.broadcast_to(sel[j:j + 1, :], (SUBLANES, blk)) > 0.0
                sj = jnp.where(keep[None], sj, NEG_BIG)
            s.append(sj)
        m8 = functools.reduce(jnp.maximum, [jnp.max(sj, axis=0) for sj in s])
        m = jnp.broadcast_to(jnp.max(m8, axis=0, keepdims=True), (SUBLANES, blk))
        return jnp.concatenate([jnp.exp2(sj - m[None]).reshape(blk, blk).astype(BF16) for sj in s], axis=0)

    nxt = scores(0)
    for i in range(nblk):
        cur = nxt
        if i + 1 < nblk:
            nxt = scores(i + 1)
        acc = _dot(vt_ref[:, 0:(i + 1) * blk], softmax(i, *cur))
        o_ref[i * blk:(i + 1) * blk, :] = (acc[0:HEAD_DIM] / acc[HEAD_DIM:HEAD_DIM + 1]).T.astype(BF16)


def _moba(q, k, v, kmean, mlp_w1, mlp_w2, layer, batch, seq):
    heads, m, _ = q.shape
    nblk = seq // MOBA_BLOCK
    step = lambda b, h: b * heads + h
    w1_in, w1_out, w1_shape = _cast_specs(mlp_w1, layer, batch * heads, step)
    w2_in, w2_out, w2_shape = _cast_specs(mlp_w2, layer, batch * heads, step)
    slopes = 2.0 ** (-8.0 * jnp.arange(1, heads + 1, dtype=F32) / heads)
    slopes = jnp.broadcast_to(slopes[:, None, None], (heads, 1, HEAD_DIM))
    spec = pl.BlockSpec((None, seq, HEAD_DIM), lambda b, h: (h, b, 0))
    return pl.pallas_call(
        functools.partial(_moba_kernel, nblk),
        grid=(batch, heads),
        in_specs=[spec, spec, spec,
                  pl.BlockSpec((1, nblk, HEAD_DIM), lambda b, h: (b, 0, h)),
                  pl.BlockSpec((1, 1, HEAD_DIM), lambda b, h: (h, 0, 0)), w1_in, w2_in],
        out_specs=[spec, w1_out, w2_out],
        out_shape=[jax.ShapeDtypeStruct((heads, m, HEAD_DIM), BF16), w1_shape, w2_shape],
        scratch_shapes=[pltpu.VMEM((HEAD_DIM + 2 * SUBLANES, seq), BF16),
                        pltpu.VMEM((nblk, MOBA_BLOCK, MOBA_BLOCK), F32)],
        compiler_params=_params("arbitrary", "arbitrary"),
        name="moba_attn",
    )(q, k, v, kmean, slopes, mlp_w1, mlp_w2)


def _tile(n, pref):
    return pref if n % pref == 0 else n


def kernel(x, a_norm, a_w_in, a_head_norm, a_w_out, lower_bounds, kv_norm, w_kv, k_norm,
           b_norm, b_w_q, b_q_norm, b_w_o, mlp_norm, mlp_w1, mlp_w2):
    batch, seq, d = x.shape
    n_a = a_w_in.shape[0]
    n_b = b_w_q.shape[0]
    assert seq % MOBA_BLOCK == 0 and d % HEAD_DIM == 0
    m = batch * seq
    tm = _tile(m, 1024)
    ts = _tile(seq, 2048)
    tr = _tile(m, 512)

    h = x.reshape(m, d)
    kb = vb = kmean = None
    for l in range(n_a + n_b):
        if l < n_a:
            q, logf, k, v, gate = _hgrn_in(h, a_norm[l], lower_bounds, a_w_in, l, tm, 256)
            a, w1b, w2b = _gla(q, k, v, logf, gate, a_head_norm[l], mlp_w1, mlp_w2, l, batch, seq, ts)
            h = _proj_res(h, a, a_w_out, l, tr)
        else:
            if l == n_a:
                half = lambda c: pl.BlockSpec((d, d), lambda i: (0, c), pipeline_mode=pl.Buffered(1))
                kb, kmean = _norm_proj(h, kv_norm, k_norm, w_kv, half(0), tr, True, True, "shared_k")
                vb, = _norm_proj(h, kv_norm, k_norm, w_kv, half(1), tr, False, False, "shared_v")
                kmean = kmean.reshape(batch, seq // MOBA_BLOCK, d)
            j = l - n_a
            q, = _norm_proj(h, b_norm[j], b_q_norm[j], b_w_q, _resident_spec(b_w_q, j), tr, True, False,
                            "moba_q", out_scale=LOG2E * HEAD_DIM ** -0.5)
            a, w1b, w2b = _moba(q, kb, vb, kmean, mlp_w1, mlp_w2, l, batch, seq)
            h = _proj_res(h, a, b_w_o, j, tr)
        h = _mlp(h, mlp_norm[l], w1b, w2b, tm, 1024)
    return h.reshape(batch, seq, d)
```

```python
import functools

import numpy as np
import jax
import jax.numpy as jnp
from jax import lax
from jax.experimental import pallas as pl
from jax.experimental.pallas import tpu as pltpu

F32 = jnp.float32
BF16 = jnp.bfloat16

HEAD_DIM = 128
MOBA_BLOCK = 256
MOBA_TOPK = 3
GLA_CHUNK = 64
EPS = 1e-6
NEG_BIG = -1e30
LB_FLOOR = 1e-30
LOG2E = 1.4426950408889634
SUBLANES = 8

V7X_VMEM_LIMIT_BYTES = 60 * 1024 * 1024


def _params(*sem):
    return pltpu.CompilerParams(dimension_semantics=sem, vmem_limit_bytes=V7X_VMEM_LIMIT_BYTES)


def _dot(a, b):
    return jnp.dot(a, b, preferred_element_type=F32)


def _wdot(a, w_ref):
    return jnp.dot(a, w_ref[...].astype(BF16), preferred_element_type=F32)


def _dot_nt(a, b):
    return lax.dot_general(a, b, (((1,), (1,)), ((), ())), preferred_element_type=F32)


def _dot_tn(a, b):
    return lax.dot_general(a, b, (((0,), (0,)), ((), ())), preferred_element_type=F32)


def _layer_spec(layer, block, index):
    return pl.BlockSpec((None,) + block, lambda i, j: (layer,) + index(i, j))


def _cast_plan(casts, steps, step_index):
    in_specs, out_specs, out_shapes, operands = [], [], [], []
    for w, layer in casts:
        if w.ndim == 2:
            w, layer = w.reshape((1,) + w.shape), 0
        rows, cols = w.shape[1] // steps, w.shape[2]
        in_specs.append(pl.BlockSpec((None, rows, cols), lambda *g, layer=layer: (layer, step_index(*g), 0)))
        out_specs.append(pl.BlockSpec((rows, cols), lambda *g: (step_index(*g), 0)))
        out_shapes.append(jax.ShapeDtypeStruct(w.shape[1:], BF16))
        operands.append(w)
    return in_specs, out_specs, out_shapes, operands


def _run_casts(in_refs, out_refs):
    for src, dst in zip(in_refs, out_refs):
        dst[...] = src[...].astype(BF16)


def _store_heads(ref, x):
    for hh in range(ref.shape[0]):
        ref[hh] = x[:, hh * HEAD_DIM:(hh + 1) * HEAD_DIM]


def _load_heads(ref):
    return jnp.concatenate([ref[hh] for hh in range(ref.shape[0])], axis=1)


def _rms(x, g):
    return x * lax.rsqrt(jnp.mean(x * x, axis=-1, keepdims=True) + EPS) * g


def _silu(x):
    return x * jax.nn.sigmoid(x)


def _head_rms(x, g):
    outs = []
    for hh in range(x.shape[1] // HEAD_DIM):
        outs.append(_rms(x[:, hh * HEAD_DIM:(hh + 1) * HEAD_DIM], g))
    return outs


def _hgrn_in_kernel(layer, h_ref, ng_ref, lb_ref, wq_ref, wf_ref, wi_ref, wg_ref,
                    q_ref, lf_ref, k_ref, v_ref, gt_ref, xn_ref):
    @pl.when(pl.program_id(1) == 0)
    def _():
        xn_ref[...] = _rms(h_ref[...], ng_ref[...]).astype(BF16)

    xn = xn_ref[...]
    pf = _wdot(xn, wf_ref)
    pq = _wdot(xn, wq_ref)
    pg = _wdot(xn, wg_ref)
    pv = _wdot(xn, wi_ref)

    lbr = lb_ref[...]
    rows = [lbr[r:r + 1, :] for r in range(lbr.shape[0])]
    mx = functools.reduce(jnp.maximum, rows)
    ex = [jnp.exp(r - mx) for r in rows]
    den = functools.reduce(lambda a, b: a + b, ex)
    p = [e / den for e in ex]
    lb = functools.reduce(lambda a, b: a + b, p[:layer + 1]) - p[0]

    e = jnp.exp(-jnp.abs(pf))
    r = 1.0 / (1.0 + e)
    er = e * r
    pos = pf >= 0.0
    one_m = 1.0 - lb
    _store_heads(lf_ref, jnp.log(jnp.maximum(lb, LB_FLOOR) + one_m * jnp.where(pos, r, er)))
    _store_heads(k_ref, (one_m * jnp.where(pos, er, r)).astype(BF16))
    _store_heads(q_ref, _silu(pq).astype(BF16))
    _store_heads(gt_ref, _silu(pg).astype(BF16))
    _store_heads(v_ref, pv.astype(BF16))


def _hgrn_in(h, norm_g, lower_bounds, w_in, layer, tm, tn):
    m, d = h.shape
    nj = d // tn
    if w_in.ndim == 3:
        wspec = lambda g: _layer_spec(layer, (d, tn), lambda i, j: (0, j + g * nj))
    else:
        wspec = lambda g: pl.BlockSpec((d, tn), lambda i, j: (0, j + g * nj))
    heads = d // HEAD_DIM
    ospec = pl.BlockSpec((tn // HEAD_DIM, tm, HEAD_DIM), lambda i, j: (j, i, 0))
    return pl.pallas_call(
        functools.partial(_hgrn_in_kernel, layer),
        grid=(m // tm, nj),
        in_specs=[pl.BlockSpec((tm, d), lambda i, j: (i, 0)),
                  pl.BlockSpec((1, d), lambda i, j: (0, 0)),
                  pl.BlockSpec((lower_bounds.shape[0], tn), lambda i, j: (0, j)),
                  wspec(0), wspec(1), wspec(2), wspec(3)],
        out_specs=[ospec] * 5,
        out_shape=[jax.ShapeDtypeStruct((heads, m, HEAD_DIM), dt) for dt in (BF16, F32, BF16, BF16, BF16)],
        scratch_shapes=[pltpu.VMEM((tm, d), BF16)],
        compiler_params=_params("arbitrary", "arbitrary"),
        name="hgrn_in",
    )(h, norm_g.reshape(1, d), lower_bounds, w_in, w_in, w_in, w_in)


def _gla_tables(c):
    r = np.arange(c)
    j = r[None, :]
    mats = [j <= r[:, None]]
    small = (4, 2, 1)
    for m in small:
        seg, pos = r // (2 * m), r % (2 * m)
        mid = (seg * 2 * m + m)[:, None]
        second = (pos >= m)[:, None]
        mats.append(np.where(second, (j >= mid) & (j <= r[:, None]), (j > r[:, None]) & (j <= mid - 1)))
    d = np.concatenate(mats, 0).astype(np.float32)
    dmat = np.concatenate([d, d, d], axis=1)
    lev = np.full((c, c), -1, np.int32)
    for li, m in enumerate(small):
        seg, pos = r // (2 * m), r % (2 * m)
        ok = (seg[:, None] == seg[None, :]) & (pos >= m)[:, None] & (pos < m)[None, :]
        lev[ok] = li
    big = []
    m = c // 2
    while m >= SUBLANES:
        t = np.concatenate([np.arange(s0 + m, s0 + 2 * m) for s0 in range(0, c, 2 * m)])
        ok = (t[:, None] // (2 * m) == r[None, :] // (2 * m)) & ((r % (2 * m)) < m)[None, :]
        big.append(ok.astype(np.float32))
        m //= 2
    return dmat, lev, np.stack(big)


def _gla_kernel(chunk, ncast, q_ref, k_ref, v_ref, g_ref, gt_ref, hg_ref, dmat_ref, lev_ref, big_ref, *refs):
    o_ref, st_ref = refs[ncast], refs[-1]

    @pl.when(pl.program_id(2) == 0)
    def _():
        st_ref[...] = jnp.zeros_like(st_ref)

    _run_casts(refs[:ncast], refs[ncast + 1:-1])

    c = chunk
    nc = q_ref.shape[0] // c
    lev = lev_ref[...]
    hg = hg_ref[...]

    g = g_ref[...] * LOG2E
    g_hi = g.astype(BF16)
    r1 = g - g_hi.astype(F32)
    g_mid = r1.astype(BF16)
    g_lo = (r1 - g_mid.astype(F32)).astype(BF16)
    lanes = lambda x: jnp.concatenate([x[i * c:(i + 1) * c] for i in range(nc)], axis=1)
    e_all = _dot(dmat_ref[...], jnp.concatenate([lanes(g_hi), lanes(g_mid), lanes(g_lo)], axis=0))

    small_p, big_p, big_tgt = [], [], []
    for ci in range(nc):
        sl = pl.ds(ci * c, c)
        cols = slice(ci * HEAD_DIM, (ci + 1) * HEAD_DIM)
        b = e_all[0:c, cols]
        q = q_ref[sl, :].astype(F32)
        k = k_ref[sl, :].astype(F32)
        ps = []
        for li in range(3):
            w = jnp.exp2(e_all[(li + 1) * c:(li + 2) * c, cols])
            ps.append(_dot_nt((q * w).astype(BF16), (k * w).astype(BF16)))
        small_p.append(ps)
        ps, tg = [], []
        m = c // 2
        while m >= SUBLANES:
            qs, ks, tgt = [], [], []
            for s0 in range(0, c, 2 * m):
                ref = b[s0 + m - 1:s0 + m, :]
                qs.append(q[s0 + m:s0 + 2 * m] * jnp.exp2(b[s0 + m:s0 + 2 * m] - ref))
                ks.append(k[s0:s0 + m] * jnp.exp2(ref - b[s0:s0 + m]))
                ks.append(k[s0 + m:s0 + 2 * m])
                tgt.extend(range((s0 + m) // SUBLANES, (s0 + 2 * m) // SUBLANES))
            ps.append(_dot_nt(jnp.concatenate(qs, 0).astype(BF16), jnp.concatenate(ks, 0).astype(BF16)))
            tg.append(tgt)
            m //= 2
        big_p.append(ps)
        big_tgt.append(tg)

    intra, qbs, upds, dcols = [], [], [], []
    for ci in range(nc):
        sl = pl.ds(ci * c, c)
        cols = slice(ci * HEAD_DIM, (ci + 1) * HEAD_DIM)
        b = e_all[0:c, cols]
        q = q_ref[sl, :].astype(F32)
        k = k_ref[sl, :].astype(F32)
        v = v_ref[sl, :]
        rows = [jnp.zeros((SUBLANES, c), F32) for _ in range(c // SUBLANES)]
        for li, p in enumerate(small_p[ci]):
            p = jnp.where(lev == li, p, 0.0)
            rows = [rw + p[i * SUBLANES:(i + 1) * SUBLANES] for i, rw in enumerate(rows)]
        for li, (p, tgt) in enumerate(zip(big_p[ci], big_tgt[ci])):
            p = p * big_ref[li]
            for n, i in enumerate(tgt):
                rows[i] = rows[i] + p[n * SUBLANES:(n + 1) * SUBLANES]
        scores = jnp.concatenate(rows, 0).astype(BF16)
        wb = jnp.exp2(b)
        we = jnp.exp2(b[c - 1:c, :] - b)
        intra.append(_dot(scores, v) + jnp.sum(q * k, axis=-1, keepdims=True) * v.astype(F32))
        qbs.append((q * wb).astype(BF16))
        upds.append(_dot_tn((k * we).astype(BF16), v))
        dcols.append(jnp.broadcast_to(wb[c - 1:c, :], (SUBLANES, HEAD_DIM)).T[:, :1])

    st = st_ref[...]
    states = []
    for ci in range(nc):
        states.append(st.astype(BF16))
        st = st * dcols[ci] + upds[ci]
    st_ref[...] = st

    for ci in range(nc):
        sl = pl.ds(ci * c, c)
        o = _dot(qbs[ci], states[ci]) + intra[ci]
        o_ref[sl, :] = (_rms(o, hg) * gt_ref[sl, :].astype(F32)).astype(BF16)


def _gla(q, k, v, logf, gate, head_gain, casts, batch, seq, ts):
    heads, m, _ = q.shape
    d = heads * HEAD_DIM
    ns = seq // ts
    dmat, lev, big = _gla_tables(GLA_CHUNK)
    spec = pl.BlockSpec((None, ts, HEAD_DIM), lambda b, h, s: (h, b * ns + s, 0))
    const = lambda a: pl.BlockSpec(a.shape, lambda b, h, s: (0,) * a.ndim)
    c_in, c_out, c_shape, c_ops = _cast_plan(casts, batch * heads * ns, lambda b, h, s: (b * heads + h) * ns + s)
    outs = pl.pallas_call(
        functools.partial(_gla_kernel, GLA_CHUNK, len(casts)),
        grid=(batch, heads, ns),
        in_specs=[spec, spec, spec, spec, spec,
                  pl.BlockSpec((1, HEAD_DIM), lambda b, h, s: (0, h)),
                  const(dmat), const(lev), const(big)] + c_in,
        out_specs=[spec] + c_out,
        out_shape=[jax.ShapeDtypeStruct((heads, m, HEAD_DIM), BF16)] + c_shape,
        scratch_shapes=[pltpu.VMEM((HEAD_DIM, HEAD_DIM), F32)],
        compiler_params=_params("arbitrary", "arbitrary", "arbitrary"),
        name="gla",
    )(q, k, v, logf, gate, head_gain.reshape(1, d), jnp.asarray(dmat, BF16), jnp.asarray(lev),
      jnp.asarray(big), *c_ops)
    return outs[0], outs[1:]


def _proj_res_kernel(h_ref, a_ref, w_ref, o_ref):
    o_ref[...] = h_ref[...] + _dot(_load_heads(a_ref), w_ref[...])


def _resident_spec(w):
    return pl.BlockSpec(w.shape, lambda i: (0, 0), pipeline_mode=pl.Buffered(1))


def _proj_res(h, a, w, tm):
    m, d = h.shape
    return pl.pallas_call(
        _proj_res_kernel,
        grid=(m // tm,),
        in_specs=[pl.BlockSpec((tm, d), lambda i: (i, 0)),
                  pl.BlockSpec((a.shape[0], tm, HEAD_DIM), lambda i: (0, i, 0)),
                  _resident_spec(w)],
        out_specs=pl.BlockSpec((tm, d), lambda i: (i, 0)),
        out_shape=jax.ShapeDtypeStruct((m, d), F32),
        compiler_params=_params("arbitrary"),
        name="proj_res",
    )(h, a, w)


def _mlp_kernel(h_ref, ng_ref, w1_ref, w2_ref, o_ref, xn_ref):
    @pl.when(pl.program_id(1) == 0)
    def _():
        x = h_ref[...]
        xn_ref[...] = _rms(x, ng_ref[...]).astype(BF16)
        o_ref[...] = x

    t = jnp.square(jnp.maximum(_dot(xn_ref[...], w1_ref[...]), 0.0)).astype(BF16)
    o_ref[...] += _dot(t, w2_ref[...])


def _mlp(h, norm_g, w1, w2, tm, tf):
    m, d = h.shape
    ff = w1.shape[-1]
    return pl.pallas_call(
        _mlp_kernel,
        grid=(m // tm, ff // tf),
        in_specs=[pl.BlockSpec((tm, d), lambda i, j: (i, 0)),
                  pl.BlockSpec((1, d), lambda i, j: (0, 0)),
                  pl.BlockSpec((d, tf), lambda i, j: (0, j)),
                  pl.BlockSpec((tf, d), lambda i, j: (j, 0))],
        out_specs=pl.BlockSpec((tm, d), lambda i, j: (i, 0)),
        out_shape=jax.ShapeDtypeStruct((m, d), F32),
        scratch_shapes=[pltpu.VMEM((tm, d), BF16)],
        compiler_params=_params("arbitrary", "arbitrary"),
        name="mlp",
    )(h, norm_g.reshape(1, d), w1, w2)


def _norm_proj_kernel(n_normed, block_mean, out_scale, h_ref, ng_ref, hn_ref, w_ref, *outs):
    tm = h_ref.shape[0]
    sub = MOBA_BLOCK if tm % MOBA_BLOCK == 0 else tm
    nh = n_normed // HEAD_DIM
    for r0 in range(0, tm, sub):
        rows = slice(r0, r0 + sub)
        y = _dot(_rms(h_ref[rows, :], ng_ref[...]).astype(BF16), w_ref[...])
        for hh, yh in enumerate(_head_rms(y[:, :n_normed], hn_ref[...])):
            outs[0][hh, rows, :] = (yh if out_scale is None else yh * out_scale).astype(BF16)
            if block_mean:
                outs[-1][0, r0 // MOBA_BLOCK:(r0 + sub) // MOBA_BLOCK, hh * HEAD_DIM:(hh + 1) * HEAD_DIM] = (
                    jnp.sum(yh.reshape(sub // MOBA_BLOCK, MOBA_BLOCK, HEAD_DIM), axis=1) * (1.0 / MOBA_BLOCK))
        for hh in range(y.shape[1] // HEAD_DIM - nh):
            outs[1][hh, rows, :] = y[:, n_normed + hh * HEAD_DIM:n_normed + (hh + 1) * HEAD_DIM].astype(BF16)


def _norm_proj(h, norm_g, head_g, w, tm, n_normed, block_mean, name, out_scale=None):
    m, d = h.shape
    n = w.shape[1]
    nblk = tm // MOBA_BLOCK
    hm = lambda cols: (pl.BlockSpec((cols // HEAD_DIM, tm, HEAD_DIM), lambda i: (0, i, 0)),
                       jax.ShapeDtypeStruct((cols // HEAD_DIM, m, HEAD_DIM), BF16))
    outs = [hm(n_normed)] + ([hm(n - n_normed)] if n > n_normed else [])
    if block_mean:
        outs.append((pl.BlockSpec((1, nblk, n_normed), lambda i: (i, 0, 0)),
                     jax.ShapeDtypeStruct((m // tm, nblk, n_normed), F32)))
    return pl.pallas_call(
        functools.partial(_norm_proj_kernel, n_normed, block_mean, out_scale),
        grid=(m // tm,),
        in_specs=[pl.BlockSpec((tm, d), lambda i: (i, 0)),
                  pl.BlockSpec((1, d), lambda i: (0, 0)),
                  pl.BlockSpec((1, HEAD_DIM), lambda i: (0, 0)),
                  _resident_spec(w)],
        out_specs=[o[0] for o in outs],
        out_shape=[o[1] for o in outs],
        compiler_params=_params("arbitrary"),
        name=name,
    )(h, norm_g.reshape(1, d), head_g.reshape(1, HEAD_DIM), w)


def _moba_kernel(nblk, ncast, q_ref, k_ref, v_ref, km_ref, slope_ref, *refs):
    o_ref, vt_ref, bias_ref = refs[ncast], refs[-2], refs[-1]
    _run_casts(refs[:ncast], refs[ncast + 1:-2])

    blk = MOBA_BLOCK
    grp = blk // SUBLANES
    for n in range(nblk):
        vt_ref[0:HEAD_DIM, n * blk:(n + 1) * blk] = v_ref[n * blk:(n + 1) * blk, :].astype(F32).T.astype(BF16)
    vt_ref[HEAD_DIM:, :] = jnp.ones((vt_ref.shape[0] - HEAD_DIM, vt_ref.shape[1]), BF16)
    slope = slope_ref[0][:, :1]
    kmean = km_ref[0].astype(BF16)
    t_idx = lax.broadcasted_iota(jnp.int32, (blk, blk), 1)
    s_idx = lax.broadcasted_iota(jnp.int32, (blk, blk), 0)
    dist0 = (t_idx - s_idx).astype(F32)
    bias_ref[0] = jnp.where(dist0 >= 0.0, (LOG2E * slope) * dist0, -NEG_BIG)
    for dlt in range(1, nblk):
        bias_ref[dlt] = (LOG2E * slope) * (dist0 + float(dlt * blk))
    n_idx = lax.broadcasted_iota(jnp.int32, (nblk, blk), 0)

    def scores(i):
        qi = q_ref[i * blk:(i + 1) * blk, :]
        sel = None
        if i > MOBA_TOPK:
            gm = jnp.where(n_idx < i, _dot_nt(kmean, qi), NEG_BIG)
            rank = jnp.zeros((nblk, blk), F32)
            for mrow in range(nblk):
                gr = gm[mrow:mrow + 1, :]
                rank = rank + ((gr > gm) | ((gr == gm) & (mrow < n_idx))).astype(F32)
            sel = ((rank < MOBA_TOPK) & (n_idx < i)).astype(F32)
        return sel, _dot_nt(k_ref[0:(i + 1) * blk, :], qi)

    def softmax(i, sel, s_all):
        s = []
        for j in range(i + 1):
            sj = s_all[j * blk:(j + 1) * blk].reshape(grp, SUBLANES, blk) \
                - bias_ref[i - j].reshape(grp, SUBLANES, blk)
            if sel is not None and j < i:
                keep = jnp.broadcast_to(sel[j:j + 1, :], (SUBLANES, blk)) > 0.0
                sj = jnp.where(keep[None], sj, NEG_BIG)
            s.append(sj)
        m8 = functools.reduce(jnp.maximum, [jnp.max(sj, axis=0) for sj in s])
        m = jnp.broadcast_to(jnp.max(m8, axis=0, keepdims=True), (SUBLANES, blk))
        return jnp.concatenate([jnp.exp2(sj - m[None]).reshape(blk, blk).astype(BF16) for sj in s], axis=0)

    nxt = scores(0)
    for i in range(nblk):
        cur = nxt
        if i + 1 < nblk:
            nxt = scores(i + 1)
        acc = _dot(vt_ref[:, 0:(i + 1) * blk], softmax(i, *cur))
        o_ref[i * blk:(i + 1) * blk, :] = (acc[0:HEAD_DIM] / acc[HEAD_DIM:HEAD_DIM + 1]).T.astype(BF16)


def _moba(q, k, v, kmean, casts, batch, seq):
    heads, m, _ = q.shape
    nblk = seq // MOBA_BLOCK
    c_in, c_out, c_shape, c_ops = _cast_plan(casts, batch * heads, lambda b, h: b * heads + h)
    slopes = 2.0 ** (-8.0 * jnp.arange(1, heads + 1, dtype=F32) / heads)
    slopes = jnp.broadcast_to(slopes[:, None, None], (heads, 1, HEAD_DIM))
    spec = pl.BlockSpec((None, seq, HEAD_DIM), lambda b, h: (h, b, 0))
    outs = pl.pallas_call(
        functools.partial(_moba_kernel, nblk, len(casts)),
        grid=(batch, heads),
        in_specs=[spec, spec, spec,
                  pl.BlockSpec((1, nblk, HEAD_DIM), lambda b, h: (b, 0, h)),
                  pl.BlockSpec((1, 1, HEAD_DIM), lambda b, h: (h, 0, 0))] + c_in,
        out_specs=[spec] + c_out,
        out_shape=[jax.ShapeDtypeStruct((heads, m, HEAD_DIM), BF16)] + c_shape,
        scratch_shapes=[pltpu.VMEM((HEAD_DIM + 2 * SUBLANES, seq), BF16),
                        pltpu.VMEM((nblk, MOBA_BLOCK, MOBA_BLOCK), F32)],
        compiler_params=_params("arbitrary", "arbitrary"),
        name="moba_attn",
    )(q, k, v, kmean, slopes, *c_ops)
    return outs[0], outs[1:]


def _tile(n, pref):
    return pref if n % pref == 0 else n


def kernel(x, a_norm, a_w_in, a_head_norm, a_w_out, lower_bounds, kv_norm, w_kv, k_norm,
           b_norm, b_w_q, b_q_norm, b_w_o, mlp_norm, mlp_w1, mlp_w2):
    batch, seq, d = x.shape
    n_a = a_w_in.shape[0]
    n_b = b_w_q.shape[0]
    assert seq % MOBA_BLOCK == 0 and d % HEAD_DIM == 0
    assert n_a >= 1
    m = batch * seq
    tm = _tile(m, 1024)
    ts = _tile(seq, 2048)
    tr = _tile(m, 512)

    h = x.reshape(m, d)
    kb = vb = kmean = None
    w_in, w_q, w_kvb = a_w_in, None, None
    for l in range(n_a + n_b):
        casts = [(mlp_w1, l), (mlp_w2, l)]
        if l < n_a:
            q, logf, k, v, gate = _hgrn_in(h, a_norm[l], lower_bounds, w_in, l, tm, 256 if w_in.ndim == 3 else 512)
            casts.append((a_w_out, l))
            if l + 1 < n_a:
                casts.append((a_w_in, l + 1))
            elif n_b:
                casts += [(w_kv, 0), (b_w_q, 0)]
            a, wb = _gla(q, k, v, logf, gate, a_head_norm[l], casts, batch, seq, ts)
            if l + 1 < n_a:
                w_in = wb[3]
            elif n_b:
                w_kvb, w_q = wb[3], wb[4]
        else:
            j = l - n_a
            if kb is None:
                kb, vb, kmean = _norm_proj(h, kv_norm, k_norm, w_kvb, tr, d, True, "shared_kv")
                kmean = kmean.reshape(batch, seq // MOBA_BLOCK, d)
            q, = _norm_proj(h, b_norm[j], b_q_norm[j], w_q, tr, d, False, "moba_q",
                            out_scale=LOG2E * HEAD_DIM ** -0.5)
            casts.append((b_w_o, j))
            if j + 1 < n_b:
                casts.append((b_w_q, j + 1))
            a, wb = _moba(q, kb, vb, kmean, casts, batch, seq)
            if j + 1 < n_b:
                w_q = wb[3]
        h = _proj_res(h, a, wb[2], tm)
        h = _mlp(h, mlp_norm[l], wb[0], wb[1], tm, 1024)
    return h.reshape(batch, seq, d)
```

```python
import functools

import numpy as np
import jax
import jax.numpy as jnp
from jax import lax
from jax.experimental import pallas as pl
from jax.experimental.pallas import tpu as pltpu

F32 = jnp.float32
BF16 = jnp.bfloat16

HEAD_DIM = 128
MOBA_BLOCK = 256
MOBA_TOPK = 3
GLA_CHUNK = 64
EPS = 1e-6
NEG_BIG = -1e30
LB_FLOOR = 1e-30
LOG2E = 1.4426950408889634
SUBLANES = 8

V7X_VMEM_LIMIT_BYTES = 60 * 1024 * 1024


def _params(*sem):
    return pltpu.CompilerParams(dimension_semantics=sem, vmem_limit_bytes=V7X_VMEM_LIMIT_BYTES)


def _dot(a, b):
    return jnp.dot(a, b, preferred_element_type=F32)


def _wdot(a, w_ref):
    return jnp.dot(a, w_ref[...].astype(BF16), preferred_element_type=F32)


def _dot_nt(a, b):
    return lax.dot_general(a, b, (((1,), (1,)), ((), ())), preferred_element_type=F32)


def _dot_tn(a, b):
    return lax.dot_general(a, b, (((0,), (0,)), ((), ())), preferred_element_type=F32)


def _layer_spec(layer, block, index):
    return pl.BlockSpec((None,) + block, lambda i, j: (layer,) + index(i, j))


def _cast_plan(casts, steps, step_index):
    in_specs, out_specs, out_shapes, operands = [], [], [], []
    for w, layer in casts:
        if w.ndim == 2:
            w, layer = w.reshape((1,) + w.shape), 0
        rows, cols = w.shape[1] // steps, w.shape[2]
        in_specs.append(pl.BlockSpec((None, rows, cols), lambda *g, layer=layer: (layer, step_index(*g), 0)))
        out_specs.append(pl.BlockSpec((rows, cols), lambda *g: (step_index(*g), 0)))
        out_shapes.append(jax.ShapeDtypeStruct(w.shape[1:], BF16))
        operands.append(w)
    return in_specs, out_specs, out_shapes, operands


def _run_casts(in_refs, out_refs):
    for src, dst in zip(in_refs, out_refs):
        dst[...] = src[...].astype(BF16)


def _store_heads(ref, x):
    for hh in range(ref.shape[0]):
        ref[hh] = x[:, hh * HEAD_DIM:(hh + 1) * HEAD_DIM]


def _load_heads(ref):
    return jnp.concatenate([ref[hh] for hh in range(ref.shape[0])], axis=1)


def _rms(x, g):
    return x * lax.rsqrt(jnp.mean(x * x, axis=-1, keepdims=True) + EPS) * g


def _silu(x):
    return x * jax.nn.sigmoid(x)


def _head_rms(x, g):
    outs = []
    for hh in range(x.shape[1] // HEAD_DIM):
        outs.append(_rms(x[:, hh * HEAD_DIM:(hh + 1) * HEAD_DIM], g))
    return outs


def _hgrn_in_kernel(layer, h_ref, ng_ref, lb_ref, wq_ref, wf_ref, wi_ref, wg_ref,
                    q_ref, lf_ref, k_ref, v_ref, gt_ref, xn_ref):
    @pl.when(pl.program_id(1) == 0)
    def _():
        xn_ref[...] = _rms(h_ref[...], ng_ref[...]).astype(BF16)

    xn = xn_ref[...]
    pf = _wdot(xn, wf_ref)
    pq = _wdot(xn, wq_ref)
    pg = _wdot(xn, wg_ref)
    pv = _wdot(xn, wi_ref)

    lbr = lb_ref[...]
    rows = [lbr[r:r + 1, :] for r in range(lbr.shape[0])]
    mx = functools.reduce(jnp.maximum, rows)
    ex = [jnp.exp(r - mx) for r in rows]
    den = functools.reduce(lambda a, b: a + b, ex)
    p = [e / den for e in ex]
    lb = functools.reduce(lambda a, b: a + b, p[:layer + 1]) - p[0]

    e = jnp.exp(-jnp.abs(pf))
    r = 1.0 / (1.0 + e)
    er = e * r
    pos = pf >= 0.0
    one_m = 1.0 - lb
    _store_heads(lf_ref, jnp.log(jnp.maximum(lb, LB_FLOOR) + one_m * jnp.where(pos, r, er)))
    _store_heads(k_ref, (one_m * jnp.where(pos, er, r)).astype(BF16))
    _store_heads(q_ref, _silu(pq).astype(BF16))
    _store_heads(gt_ref, _silu(pg).astype(BF16))
    _store_heads(v_ref, pv.astype(BF16))


def _hgrn_in(h, norm_g, lower_bounds, w_in, layer, tm, tn):
    m, d = h.shape
    nj = d // tn
    if w_in.ndim == 3:
        wspec = lambda g: _layer_spec(layer, (d, tn), lambda i, j: (0, j + g * nj))
    else:
        wspec = lambda g: pl.BlockSpec((d, tn), lambda i, j: (0, j + g * nj))
    heads = d // HEAD_DIM
    ospec = pl.BlockSpec((tn // HEAD_DIM, tm, HEAD_DIM), lambda i, j: (j, i, 0))
    return pl.pallas_call(
        functools.partial(_hgrn_in_kernel, layer),
        grid=(m // tm, nj),
        in_specs=[pl.BlockSpec((tm, d), lambda i, j: (i, 0)),
                  pl.BlockSpec((1, d), lambda i, j: (0, 0)),
                  pl.BlockSpec((lower_bounds.shape[0], tn), lambda i, j: (0, j)),
                  wspec(0), wspec(1), wspec(2), wspec(3)],
        out_specs=[ospec] * 5,
        out_shape=[jax.ShapeDtypeStruct((heads, m, HEAD_DIM), dt) for dt in (BF16, F32, BF16, BF16, BF16)],
        scratch_shapes=[pltpu.VMEM((tm, d), BF16)],
        compiler_params=_params("arbitrary", "arbitrary"),
        name="hgrn_in",
    )(h, norm_g.reshape(1, d), lower_bounds, w_in, w_in, w_in, w_in)


def _gla_tables(c):
    r = np.arange(c)
    j = r[None, :]
    mats = [j <= r[:, None]]
    small = (4, 2, 1)
    for m in small:
        seg, pos = r // (2 * m), r % (2 * m)
        mid = (seg * 2 * m + m)[:, None]
        second = (pos >= m)[:, None]
        mats.append(np.where(second, (j >= mid) & (j <= r[:, None]), (j > r[:, None]) & (j <= mid - 1)))
    d = np.concatenate(mats, 0).astype(np.float32)
    dmat = np.concatenate([d, d, d], axis=1)
    lev = np.full((c, c), -1, np.int32)
    for li, m in enumerate(small):
        seg, pos = r // (2 * m), r % (2 * m)
        ok = (seg[:, None] == seg[None, :]) & (pos >= m)[:, None] & (pos < m)[None, :]
        lev[ok] = li
    big = []
    m = c // 2
    while m >= SUBLANES:
        t = np.concatenate([np.arange(s0 + m, s0 + 2 * m) for s0 in range(0, c, 2 * m)])
        ok = (t[:, None] // (2 * m) == r[None, :] // (2 * m)) & ((r % (2 * m)) < m)[None, :]
        big.append(ok.astype(np.float32))
        m //= 2
    return dmat, lev, np.stack(big)


def _gla_kernel(chunk, ncast, q_ref, k_ref, v_ref, g_ref, gt_ref, hg_ref, dmat_ref, lev_ref, big_ref, *refs):
    o_ref, st_ref = refs[ncast], refs[-1]

    @pl.when(pl.program_id(2) == 0)
    def _():
        st_ref[...] = jnp.zeros_like(st_ref)

    _run_casts(refs[:ncast], refs[ncast + 1:-1])

    c = chunk
    nc = q_ref.shape[0] // c
    lev = lev_ref[...]
    hg = hg_ref[...]

    g = g_ref[...] * LOG2E
    g_hi = g.astype(BF16)
    r1 = g - g_hi.astype(F32)
    g_mid = r1.astype(BF16)
    g_lo = (r1 - g_mid.astype(F32)).astype(BF16)
    lanes = lambda x: jnp.concatenate([x[i * c:(i + 1) * c] for i in range(nc)], axis=1)
    e_all = _dot(dmat_ref[...], jnp.concatenate([lanes(g_hi), lanes(g_mid), lanes(g_lo)], axis=0))

    small_p, big_p, big_tgt = [], [], []
    for ci in range(nc):
        sl = pl.ds(ci * c, c)
        cols = slice(ci * HEAD_DIM, (ci + 1) * HEAD_DIM)
        b = e_all[0:c, cols]
        q = q_ref[sl, :].astype(F32)
        k = k_ref[sl, :].astype(F32)
        ps = []
        for li in range(3):
            w = jnp.exp2(e_all[(li + 1) * c:(li + 2) * c, cols])
            ps.append(_dot_nt((q * w).astype(BF16), (k * w).astype(BF16)))
        small_p.append(ps)
        ps, tg = [], []
        m = c // 2
        while m >= SUBLANES:
            qs, ks, tgt = [], [], []
            for s0 in range(0, c, 2 * m):
                ref = b[s0 + m - 1:s0 + m, :]
                qs.append(q[s0 + m:s0 + 2 * m] * jnp.exp2(b[s0 + m:s0 + 2 * m] - ref))
                ks.append(k[s0:s0 + m] * jnp.exp2(ref - b[s0:s0 + m]))
                ks.append(k[s0 + m:s0 + 2 * m])
                tgt.extend(range((s0 + m) // SUBLANES, (s0 + 2 * m) // SUBLANES))
            ps.append(_dot_nt(jnp.concatenate(qs, 0).astype(BF16), jnp.concatenate(ks, 0).astype(BF16)))
            tg.append(tgt)
            m //= 2
        big_p.append(ps)
        big_tgt.append(tg)

    intra, qbs, upds, dcols = [], [], [], []
    for ci in range(nc):
        sl = pl.ds(ci * c, c)
        cols = slice(ci * HEAD_DIM, (ci + 1) * HEAD_DIM)
        b = e_all[0:c, cols]
        q = q_ref[sl, :].astype(F32)
        k = k_ref[sl, :].astype(F32)
        v = v_ref[sl, :]
        rows = [jnp.zeros((SUBLANES, c), F32) for _ in range(c // SUBLANES)]
        for li, p in enumerate(small_p[ci]):
            p = jnp.where(lev == li, p, 0.0)
            rows = [rw + p[i * SUBLANES:(i + 1) * SUBLANES] for i, rw in enumerate(rows)]
        for li, (p, tgt) in enumerate(zip(big_p[ci], big_tgt[ci])):
            p = p * big_ref[li]
            for n, i in enumerate(tgt):
                rows[i] = rows[i] + p[n * SUBLANES:(n + 1) * SUBLANES]
        scores = jnp.concatenate(rows, 0).astype(BF16)
        wb = jnp.exp2(b)
        we = jnp.exp2(b[c - 1:c, :] - b)
        intra.append(_dot(scores, v) + jnp.sum(q * k, axis=-1, keepdims=True) * v.astype(F32))
        qbs.append((q * wb).astype(BF16))
        upds.append(_dot_tn((k * we).astype(BF16), v))
        dcols.append(jnp.broadcast_to(wb[c - 1:c, :], (SUBLANES, HEAD_DIM)).T[:, :1])

    st = st_ref[...]
    states = []
    for ci in range(nc):
        states.append(st.astype(BF16))
        st = st * dcols[ci] + upds[ci]
    st_ref[...] = st

    for ci in range(nc):
        sl = pl.ds(ci * c, c)
        o = _dot(qbs[ci], states[ci]) + intra[ci]
        o_ref[sl, :] = (_rms(o, hg) * gt_ref[sl, :].astype(F32)).astype(BF16)


def _gla(q, k, v, logf, gate, head_gain, casts, batch, seq, ts):
    heads, m, _ = q.shape
    d = heads * HEAD_DIM
    ns = seq // ts
    dmat, lev, big = _gla_tables(GLA_CHUNK)
    spec = pl.BlockSpec((None, ts, HEAD_DIM), lambda b, h, s: (h, b * ns + s, 0))
    const = lambda a: pl.BlockSpec(a.shape, lambda b, h, s: (0,) * a.ndim)
    c_in, c_out, c_shape, c_ops = _cast_plan(casts, batch * heads * ns, lambda b, h, s: (b * heads + h) * ns + s)
    outs = pl.pallas_call(
        functools.partial(_gla_kernel, GLA_CHUNK, len(casts)),
        grid=(batch, heads, ns),
        in_specs=[spec, spec, spec, spec, spec,
                  pl.BlockSpec((1, HEAD_DIM), lambda b, h, s: (0, h)),
                  const(dmat), const(lev), const(big)] + c_in,
        out_specs=[spec] + c_out,
        out_shape=[jax.ShapeDtypeStruct((heads, m, HEAD_DIM), BF16)] + c_shape,
        scratch_shapes=[pltpu.VMEM((HEAD_DIM, HEAD_DIM), F32)],
        compiler_params=_params("arbitrary", "arbitrary", "arbitrary"),
        name="gla",
    )(q, k, v, logf, gate, head_gain.reshape(1, d), jnp.asarray(dmat, BF16), jnp.asarray(lev),
      jnp.asarray(big), *c_ops)
    return outs[0], outs[1:]


def _proj_res_kernel(h_ref, a_ref, w_ref, o_ref):
    o_ref[...] = h_ref[...] + _dot(_load_heads(a_ref), w_ref[...])


def _resident_spec(w):
    return pl.BlockSpec(w.shape, lambda i: (0, 0), pipeline_mode=pl.Buffered(1))


def _proj_res(h, a, w, tm):
    m, d = h.shape
    return pl.pallas_call(
        _proj_res_kernel,
        grid=(m // tm,),
        in_specs=[pl.BlockSpec((tm, d), lambda i: (i, 0)),
                  pl.BlockSpec((a.shape[0], tm, HEAD_DIM), lambda i: (0, i, 0)),
                  _resident_spec(w)],
        out_specs=pl.BlockSpec((tm, d), lambda i: (i, 0)),
        out_shape=jax.ShapeDtypeStruct((m, d), F32),
        compiler_params=_params("arbitrary"),
        name="proj_res",
    )(h, a, w)


def _mlp_kernel(h_ref, ng_ref, w1_ref, w2_ref, o_ref, xn_ref):
    @pl.when(pl.program_id(1) == 0)
    def _():
        x = h_ref[...]
        xn_ref[...] = _rms(x, ng_ref[...]).astype(BF16)
        o_ref[...] = x

    t = jnp.square(jnp.maximum(_dot(xn_ref[...], w1_ref[...]), 0.0)).astype(BF16)
    o_ref[...] += _dot(t, w2_ref[...])


def _mlp(h, norm_g, w1, w2, tm, tf):
    m, d = h.shape
    ff = w1.shape[-1]
    return pl.pallas_call(
        _mlp_kernel,
        grid=(m // tm, ff // tf),
        in_specs=[pl.BlockSpec((tm, d), lambda i, j: (i, 0)),
                  pl.BlockSpec((1, d), lambda i, j: (0, 0)),
                  pl.BlockSpec((d, tf), lambda i, j: (0, j)),
                  pl.BlockSpec((tf, d), lambda i, j: (j, 0))],
        out_specs=pl.BlockSpec((tm, d), lambda i, j: (i, 0)),
        out_shape=jax.ShapeDtypeStruct((m, d), F32),
        scratch_shapes=[pltpu.VMEM((tm, d), BF16)],
        compiler_params=_params("arbitrary", "arbitrary"),
        name="mlp",
    )(h, norm_g.reshape(1, d), w1, w2)


def _norm_proj_kernel(n_normed, block_mean, out_scale, h_ref, ng_ref, hn_ref, w_ref, *outs):
    tm = h_ref.shape[0]
    sub = MOBA_BLOCK if tm % MOBA_BLOCK == 0 else tm
    nh = n_normed // HEAD_DIM
    for r0 in range(0, tm, sub):
        rows = slice(r0, r0 + sub)
        y = _dot(_rms(h_ref[rows, :], ng_ref[...]).astype(BF16), w_ref[...])
        for hh, yh in enumerate(_head_rms(y[:, :n_normed], hn_ref[...])):
            outs[0][hh, rows, :] = (yh if out_scale is None else yh * out_scale).astype(BF16)
            if block_mean:
                outs[-1][0, r0 // MOBA_BLOCK:(r0 + sub) // MOBA_BLOCK, hh * HEAD_DIM:(hh + 1) * HEAD_DIM] = (
                    jnp.sum(yh.reshape(sub // MOBA_BLOCK, MOBA_BLOCK, HEAD_DIM), axis=1) * (1.0 / MOBA_BLOCK))
        for hh in range(y.shape[1] // HEAD_DIM - nh):
            outs[1][hh, rows, :] = y[:, n_normed + hh * HEAD_DIM:n_normed + (hh + 1) * HEAD_DIM].astype(BF16)


def _norm_proj(h, norm_g, head_g, w, tm, n_normed, block_mean, name, out_scale=None):
    m, d = h.shape
    n = w.shape[1]
    nblk = tm // MOBA_BLOCK
    hm = lambda cols: (pl.BlockSpec((cols // HEAD_DIM, tm, HEAD_DIM), lambda i: (0, i, 0)),
                       jax.ShapeDtypeStruct((cols // HEAD_DIM, m, HEAD_DIM), BF16))
    outs = [hm(n_normed)] + ([hm(n - n_normed)] if n > n_normed else [])
    if block_mean:
        outs.append((pl.BlockSpec((1, nblk, n_normed), lambda i: (i, 0, 0)),
                     jax.ShapeDtypeStruct((m // tm, nblk, n_normed), F32)))
    return pl.pallas_call(
        functools.partial(_norm_proj_kernel, n_normed, block_mean, out_scale),
        grid=(m // tm,),
        in_specs=[pl.BlockSpec((tm, d), lambda i: (i, 0)),
                  pl.BlockSpec((1, d), lambda i: (0, 0)),
                  pl.BlockSpec((1, HEAD_DIM), lambda i: (0, 0)),
                  _resident_spec(w)],
        out_specs=[o[0] for o in outs],
        out_shape=[o[1] for o in outs],
        compiler_params=_params("arbitrary"),
        name=name,
    )(h, norm_g.reshape(1, d), head_g.reshape(1, HEAD_DIM), w)


def _moba_kernel(nblk, ncast, q_ref, k_ref, v_ref, km_ref, slope_ref, *refs):
    o_ref, vt_ref, bias_ref = refs[ncast], refs[-2], refs[-1]
    _run_casts(refs[:ncast], refs[ncast + 1:-2])

    blk = MOBA_BLOCK
    grp = blk // SUBLANES
    for n in range(nblk):
        vt_ref[0:HEAD_DIM, n * blk:(n + 1) * blk] = v_ref[n * blk:(n + 1) * blk, :].astype(F32).T.astype(BF16)
    vt_ref[HEAD_DIM:, :] = jnp.ones((vt_ref.shape[0] - HEAD_DIM, vt_ref.shape[1]), BF16)
    slope = slope_ref[0][:, :1]
    kmean = km_ref[0].astype(BF16)
    t_idx = lax.broadcasted_iota(jnp.int32, (blk, blk), 1)
    s_idx = lax.broadcasted_iota(jnp.int32, (blk, blk), 0)
    dist0 = (t_idx - s_idx).astype(F32)
    bias_ref[0] = jnp.where(dist0 >= 0.0, (LOG2E * slope) * dist0, -NEG_BIG)
    for dlt in range(1, nblk):
        bias_ref[dlt] = (LOG2E * slope) * (dist0 + float(dlt * blk))
    n_idx = lax.broadcasted_iota(jnp.int32, (nblk, blk), 0)

    def scores(i):
        qi = q_ref[i * blk:(i + 1) * blk, :]
        sel = None
        if i > MOBA_TOPK:
            gm = jnp.where(n_idx < i, _dot_nt(kmean, qi), NEG_BIG)
            rank = jnp.zeros((nblk, blk), F32)
            for mrow in range(nblk):
                gr = gm[mrow:mrow + 1, :]
                rank = rank + ((gr > gm) | ((gr == gm) & (mrow < n_idx))).astype(F32)
            sel = ((rank < MOBA_TOPK) & (n_idx < i)).astype(F32)
        return sel, _dot_nt(k_ref[0:(i + 1) * blk, :], qi)

    def softmax(i, sel, s_all):
        s = []
        for j in range(i + 1):
            sj = s_all[j * blk:(j + 1) * blk].reshape(grp, SUBLANES, blk) \
                - bias_ref[i - j].reshape(grp, SUBLANES, blk)
            if sel is not None and j < i:
                keep = jnp.broadcast_to(sel[j:j + 1, :], (SUBLANES, blk)) > 0.0
                sj = jnp.where(keep[None], sj, NEG_BIG)
            s.append(sj)
        m8 = functools.reduce(jnp.maximum, [jnp.max(sj, axis=0) for sj in s])
        m = jnp.broadcast_to(jnp.max(m8, axis=0, keepdims=True), (SUBLANES, blk))
        return jnp.concatenate([jnp.exp2(sj - m[None]).reshape(blk, blk).astype(BF16) for sj in s], axis=0)

    order = list(range(nblk - 1, -1, -1))
    depth = 4
    pend = [scores(i) for i in order[:depth]]
    for n, i in enumerate(order):
        cur = pend.pop(0)
        if n + depth < nblk:
            pend.append(scores(order[n + depth]))
        acc = _dot(vt_ref[:, 0:(i + 1) * blk], softmax(i, *cur))
        o_ref[i * blk:(i + 1) * blk, :] = (acc[0:HEAD_DIM] / acc[HEAD_DIM:HEAD_DIM + 1]).T.astype(BF16)


def _moba(q, k, v, kmean, casts, batch, seq):
    heads, m, _ = q.shape
    nblk = seq // MOBA_BLOCK
    c_in, c_out, c_shape, c_ops = _cast_plan(casts, batch * heads, lambda b, h: b * heads + h)
    slopes = 2.0 ** (-8.0 * jnp.arange(1, heads + 1, dtype=F32) / heads)
    slopes = jnp.broadcast_to(slopes[:, None, None], (heads, 1, HEAD_DIM))
    spec = pl.BlockSpec((None, seq, HEAD_DIM), lambda b, h: (h, b, 0))
    outs = pl.pallas_call(
        functools.partial(_moba_kernel, nblk, len(casts)),
        grid=(batch, heads),
        in_specs=[spec, spec, spec,
                  pl.BlockSpec((1, nblk, HEAD_DIM), lambda b, h: (b, 0, h)),
                  pl.BlockSpec((1, 1, HEAD_DIM), lambda b, h: (h, 0, 0))] + c_in,
        out_specs=[spec] + c_out,
        out_shape=[jax.ShapeDtypeStruct((heads, m, HEAD_DIM), BF16)] + c_shape,
        scratch_shapes=[pltpu.VMEM((HEAD_DIM + 2 * SUBLANES, seq), BF16),
                        pltpu.VMEM((nblk, MOBA_BLOCK, MOBA_BLOCK), F32)],
        compiler_params=_params("arbitrary", "arbitrary"),
        name="moba_attn",
    )(q, k, v, kmean, slopes, *c_ops)
    return outs[0], outs[1:]


def _tile(n, pref):
    return pref if n % pref == 0 else n


def kernel(x, a_norm, a_w_in, a_head_norm, a_w_out, lower_bounds, kv_norm, w_kv, k_norm,
           b_norm, b_w_q, b_q_norm, b_w_o, mlp_norm, mlp_w1, mlp_w2):
    batch, seq, d = x.shape
    n_a = a_w_in.shape[0]
    n_b = b_w_q.shape[0]
    assert seq % MOBA_BLOCK == 0 and d % HEAD_DIM == 0
    assert n_a >= 1
    m = batch * seq
    tm = _tile(m, 1024)
    ts = _tile(seq, 2048)
    tr = _tile(m, 512)

    h = x.reshape(m, d)
    kb = vb = kmean = None
    w_in, w_q, w_kvb = a_w_in, None, None
    for l in range(n_a + n_b):
        casts = [(mlp_w1, l), (mlp_w2, l)]
        if l < n_a:
            q, logf, k, v, gate = _hgrn_in(h, a_norm[l], lower_bounds, w_in, l, tm, 256 if w_in.ndim == 3 else 512)
            casts.append((a_w_out, l))
            if l + 1 < n_a:
                casts.append((a_w_in, l + 1))
            elif n_b:
                casts += [(w_kv, 0), (b_w_q, 0)]
            a, wb = _gla(q, k, v, logf, gate, a_head_norm[l], casts, batch, seq, ts)
            if l + 1 < n_a:
                w_in = wb[3]
            elif n_b:
                w_kvb, w_q = wb[3], wb[4]
        else:
            j = l - n_a
            if kb is None:
                kb, vb, kmean = _norm_proj(h, kv_norm, k_norm, w_kvb, tr, d, True, "shared_kv")
                kmean = kmean.reshape(batch, seq // MOBA_BLOCK, d)
            q, = _norm_proj(h, b_norm[j], b_q_norm[j], w_q, tr, d, False, "moba_q",
                            out_scale=LOG2E * HEAD_DIM ** -0.5)
            casts.append((b_w_o, j))
            if j + 1 < n_b:
                casts.append((b_w_q, j + 1))
            a, wb = _moba(q, kb, vb, kmean, casts, batch, seq)
            if j + 1 < n_b:
                w_q = wb[3]
        h = _proj_res(h, a, wb[2], tm)
        h = _mlp(h, mlp_norm[l], wb[0], wb[1], tm, 1024)
    return h.reshape(batch, seq, d)
```

```python
import functools

import numpy as np
import jax
import jax.numpy as jnp
from jax import lax
from jax.experimental import pallas as pl
from jax.experimental.pallas import tpu as pltpu

F32 = jnp.float32
BF16 = jnp.bfloat16

HEAD_DIM = 128
MOBA_BLOCK = 256
MOBA_TOPK = 3
GLA_CHUNK = 64
EPS = 1e-6
NEG_BIG = -1e30
LB_FLOOR = 1e-30
LOG2E = 1.4426950408889634
SUBLANES = 8
HEADS_PER_STEP = 2

V7X_VMEM_LIMIT_BYTES = 60 * 1024 * 1024


def _params(*sem):
    return pltpu.CompilerParams(dimension_semantics=sem, vmem_limit_bytes=V7X_VMEM_LIMIT_BYTES)


def _dot(a, b):
    return jnp.dot(a, b, preferred_element_type=F32)


def _wdot(a, w_ref):
    return jnp.dot(a, w_ref[...].astype(BF16), preferred_element_type=F32)


def _dot_nt(a, b):
    return lax.dot_general(a, b, (((1,), (1,)), ((), ())), preferred_element_type=F32)


def _dot_tn(a, b):
    return lax.dot_general(a, b, (((0,), (0,)), ((), ())), preferred_element_type=F32)


def _layer_spec(layer, block, index):
    return pl.BlockSpec((None,) + block, lambda i, j: (layer,) + index(i, j))


def _cast_plan(casts, steps, step_index):
    in_specs, out_specs, out_shapes, operands = [], [], [], []
    for w, layer in casts:
        if w.ndim == 2:
            w, layer = w.reshape((1,) + w.shape), 0
        rows, cols = w.shape[1] // steps, w.shape[2]
        in_specs.append(pl.BlockSpec((None, rows, cols), lambda *g, layer=layer: (layer, step_index(*g), 0)))
        out_specs.append(pl.BlockSpec((rows, cols), lambda *g: (step_index(*g), 0)))
        out_shapes.append(jax.ShapeDtypeStruct(w.shape[1:], BF16))
        operands.append(w)
    return in_specs, out_specs, out_shapes, operands


def _run_casts(in_refs, out_refs):
    for src, dst in zip(in_refs, out_refs):
        dst[...] = src[...].astype(BF16)


def _store_heads(ref, x):
    for hh in range(ref.shape[0]):
        ref[hh] = x[:, hh * HEAD_DIM:(hh + 1) * HEAD_DIM]


def _load_heads(ref):
    return jnp.concatenate([ref[hh] for hh in range(ref.shape[0])], axis=1)


def _rms(x, g):
    return x * lax.rsqrt(jnp.mean(x * x, axis=-1, keepdims=True) + EPS) * g


def _silu(x):
    return x * jax.nn.sigmoid(x)


def _head_rms(x, g):
    outs = []
    for hh in range(x.shape[1] // HEAD_DIM):
        outs.append(_rms(x[:, hh * HEAD_DIM:(hh + 1) * HEAD_DIM], g))
    return outs


def _hgrn_in_kernel(layer, h_ref, ng_ref, lb_ref, wq_ref, wf_ref, wi_ref, wg_ref,
                    q_ref, lf_ref, k_ref, v_ref, gt_ref, xn_ref):
    @pl.when(pl.program_id(1) == 0)
    def _():
        xn_ref[...] = _rms(h_ref[...], ng_ref[...]).astype(BF16)

    xn = xn_ref[...]
    pf = _wdot(xn, wf_ref)
    pq = _wdot(xn, wq_ref)
    pg = _wdot(xn, wg_ref)
    pv = _wdot(xn, wi_ref)

    lbr = lb_ref[...]
    rows = [lbr[r:r + 1, :] for r in range(lbr.shape[0])]
    mx = functools.reduce(jnp.maximum, rows)
    ex = [jnp.exp(r - mx) for r in rows]
    den = functools.reduce(lambda a, b: a + b, ex)
    p = [e / den for e in ex]
    lb = functools.reduce(lambda a, b: a + b, p[:layer + 1]) - p[0]

    e = jnp.exp(-jnp.abs(pf))
    r = 1.0 / (1.0 + e)
    er = e * r
    pos = pf >= 0.0
    one_m = 1.0 - lb
    _store_heads(lf_ref, jnp.log(jnp.maximum(lb, LB_FLOOR) + one_m * jnp.where(pos, r, er)))
    _store_heads(k_ref, (one_m * jnp.where(pos, er, r)).astype(BF16))
    _store_heads(q_ref, _silu(pq).astype(BF16))
    _store_heads(gt_ref, _silu(pg).astype(BF16))
    _store_heads(v_ref, pv.astype(BF16))


def _hgrn_in(h, norm_g, lower_bounds, w_in, layer, tm, tn):
    m, d = h.shape
    nj = d // tn
    if w_in.ndim == 3:
        wspec = lambda g: _layer_spec(layer, (d, tn), lambda i, j: (0, j + g * nj))
    else:
        wspec = lambda g: pl.BlockSpec((d, tn), lambda i, j: (0, j + g * nj))
    heads = d // HEAD_DIM
    ospec = pl.BlockSpec((tn // HEAD_DIM, tm, HEAD_DIM), lambda i, j: (j, i, 0))
    return pl.pallas_call(
        functools.partial(_hgrn_in_kernel, layer),
        grid=(m // tm, nj),
        in_specs=[pl.BlockSpec((tm, d), lambda i, j: (i, 0)),
                  pl.BlockSpec((1, d), lambda i, j: (0, 0)),
                  pl.BlockSpec((lower_bounds.shape[0], tn), lambda i, j: (0, j)),
                  wspec(0), wspec(1), wspec(2), wspec(3)],
        out_specs=[ospec] * 5,
        out_shape=[jax.ShapeDtypeStruct((heads, m, HEAD_DIM), dt) for dt in (BF16, F32, BF16, BF16, BF16)],
        scratch_shapes=[pltpu.VMEM((tm, d), BF16)],
        compiler_params=_params("arbitrary", "arbitrary"),
        name="hgrn_in",
    )(h, norm_g.reshape(1, d), lower_bounds, w_in, w_in, w_in, w_in)


def _gla_tables(c):
    r = np.arange(c)
    j = r[None, :]
    mats = [j <= r[:, None]]
    small = (4, 2, 1)
    for m in small:
        seg, pos = r // (2 * m), r % (2 * m)
        mid = (seg * 2 * m + m)[:, None]
        second = (pos >= m)[:, None]
        mats.append(np.where(second, (j >= mid) & (j <= r[:, None]), (j > r[:, None]) & (j <= mid - 1)))
    d = np.concatenate(mats, 0).astype(np.float32)
    dmat = np.concatenate([d, d, d], axis=1)
    lev = np.full((c, c), -1, np.int32)
    for li, m in enumerate(small):
        seg, pos = r // (2 * m), r % (2 * m)
        ok = (seg[:, None] == seg[None, :]) & (pos >= m)[:, None] & (pos < m)[None, :]
        lev[ok] = li
    big = []
    m = c // 2
    while m >= SUBLANES:
        t = np.concatenate([np.arange(s0 + m, s0 + 2 * m) for s0 in range(0, c, 2 * m)])
        ok = (t[:, None] // (2 * m) == r[None, :] // (2 * m)) & ((r % (2 * m)) < m)[None, :]
        big.append(ok.astype(np.float32))
        m //= 2
    return dmat, lev, np.stack(big)


def _gla_kernel(chunk, ncast, q_ref, k_ref, v_ref, g_ref, gt_ref, hg_ref, dmat_ref, lev_ref, big_ref, *refs):
    o_ref, st_ref = refs[ncast], refs[-1]

    @pl.when(pl.program_id(2) == 0)
    def _():
        st_ref[...] = jnp.zeros_like(st_ref)

    _run_casts(refs[:ncast], refs[ncast + 1:-1])
    for hd in range(q_ref.shape[0]):
        _gla_head(chunk, q_ref.at[hd], k_ref.at[hd], v_ref.at[hd], g_ref.at[hd], gt_ref.at[hd],
                  hg_ref[:, hd * HEAD_DIM:(hd + 1) * HEAD_DIM], dmat_ref, lev_ref, big_ref, o_ref.at[hd], st_ref.at[hd])


def _gla_head(chunk, q_ref, k_ref, v_ref, g_ref, gt_ref, hg, dmat_ref, lev_ref, big_ref, o_ref, st_ref):
    c = chunk
    nc = q_ref.shape[0] // c
    lev = lev_ref[...]

    g = g_ref[...] * LOG2E
    g_hi = g.astype(BF16)
    r1 = g - g_hi.astype(F32)
    g_mid = r1.astype(BF16)
    g_lo = (r1 - g_mid.astype(F32)).astype(BF16)
    lanes = lambda x: jnp.concatenate([x[i * c:(i + 1) * c] for i in range(nc)], axis=1)
    e_all = _dot(dmat_ref[...], jnp.concatenate([lanes(g_hi), lanes(g_mid), lanes(g_lo)], axis=0))

    small_p, big_p, big_tgt = [], [], []
    for ci in range(nc):
        sl = pl.ds(ci * c, c)
        cols = slice(ci * HEAD_DIM, (ci + 1) * HEAD_DIM)
        b = e_all[0:c, cols]
        q = q_ref[sl, :].astype(F32)
        k = k_ref[sl, :].astype(F32)
        ps = []
        for li in range(3):
            w = jnp.exp2(e_all[(li + 1) * c:(li + 2) * c, cols])
            ps.append(_dot_nt((q * w).astype(BF16), (k * w).astype(BF16)))
        small_p.append(ps)
        ps, tg = [], []
        m = c // 2
        while m >= SUBLANES:
            qs, ks, tgt = [], [], []
            for s0 in range(0, c, 2 * m):
                ref = b[s0 + m - 1:s0 + m, :]
                qs.append(q[s0 + m:s0 + 2 * m] * jnp.exp2(b[s0 + m:s0 + 2 * m] - ref))
                ks.append(k[s0:s0 + m] * jnp.exp2(ref - b[s0:s0 + m]))
                ks.append(k[s0 + m:s0 + 2 * m])
                tgt.extend(range((s0 + m) // SUBLANES, (s0 + 2 * m) // SUBLANES))
            ps.append(_dot_nt(jnp.concatenate(qs, 0).astype(BF16), jnp.concatenate(ks, 0).astype(BF16)))
            tg.append(tgt)
            m //= 2
        big_p.append(ps)
        big_tgt.append(tg)

    intra, qbs, upds, dcols = [], [], [], []
    for ci in range(nc):
        sl = pl.ds(ci * c, c)
        cols = slice(ci * HEAD_DIM, (ci + 1) * HEAD_DIM)
        b = e_all[0:c, cols]
        q = q_ref[sl, :].astype(F32)
        k = k_ref[sl, :].astype(F32)
        v = v_ref[sl, :]
        rows = [jnp.zeros((SUBLANES, c), F32) for _ in range(c // SUBLANES)]
        for li, p in enumerate(small_p[ci]):
            p = jnp.where(lev == li, p, 0.0)
            rows = [rw + p[i * SUBLANES:(i + 1) * SUBLANES] for i, rw in enumerate(rows)]
        for li, (p, tgt) in enumerate(zip(big_p[ci], big_tgt[ci])):
            p = p * big_ref[li]
            for n, i in enumerate(tgt):
                rows[i] = rows[i] + p[n * SUBLANES:(n + 1) * SUBLANES]
        scores = jnp.concatenate(rows, 0).astype(BF16)
        wb = jnp.exp2(b)
        we = jnp.exp2(b[c - 1:c, :] - b)
        intra.append(_dot(scores, v) + jnp.sum(q * k, axis=-1, keepdims=True) * v.astype(F32))
        qbs.append((q * wb).astype(BF16))
        upds.append(_dot_tn((k * we).astype(BF16), v))
        dcols.append(jnp.broadcast_to(wb[c - 1:c, :], (SUBLANES, HEAD_DIM)).T[:, :1])

    st = st_ref[...]
    states = []
    for ci in range(nc):
        states.append(st.astype(BF16))
        st = st * dcols[ci] + upds[ci]
    st_ref[...] = st

    for ci in range(nc):
        sl = pl.ds(ci * c, c)
        o = _dot(qbs[ci], states[ci]) + intra[ci]
        o_ref[sl, :] = (_rms(o, hg) * gt_ref[sl, :].astype(F32)).astype(BF16)


def _gla(q, k, v, logf, gate, head_gain, casts, batch, seq, ts):
    heads, m, _ = q.shape
    d = heads * HEAD_DIM
    ns = seq // ts
    dmat, lev, big = _gla_tables(GLA_CHUNK)
    hps = HEADS_PER_STEP
    hsteps = heads // hps
    spec = pl.BlockSpec((hps, ts, HEAD_DIM), lambda b, h, s: (h, b * ns + s, 0))
    const = lambda a: pl.BlockSpec(a.shape, lambda b, h, s: (0,) * a.ndim)
    c_in, c_out, c_shape, c_ops = _cast_plan(casts, batch * hsteps * ns, lambda b, h, s: (b * hsteps + h) * ns + s)
    outs = pl.pallas_call(
        functools.partial(_gla_kernel, GLA_CHUNK, len(casts)),
        grid=(batch, hsteps, ns),
        in_specs=[spec, spec, spec, spec, spec,
                  pl.BlockSpec((1, hps * HEAD_DIM), lambda b, h, s: (0, h)),
                  const(dmat), const(lev), const(big)] + c_in,
        out_specs=[spec] + c_out,
        out_shape=[jax.ShapeDtypeStruct((heads, m, HEAD_DIM), BF16)] + c_shape,
        scratch_shapes=[pltpu.VMEM((hps, HEAD_DIM, HEAD_DIM), F32)],
        compiler_params=_params("arbitrary", "arbitrary", "arbitrary"),
        name="gla",
    )(q, k, v, logf, gate, head_gain.reshape(1, d), jnp.asarray(dmat, BF16), jnp.asarray(lev),
      jnp.asarray(big), *c_ops)
    return outs[0], outs[1:]


def _proj_res_kernel(h_ref, a_ref, w_ref, o_ref):
    o_ref[...] = h_ref[...] + _dot(_load_heads(a_ref), w_ref[...])


def _resident_spec(w):
    return pl.BlockSpec(w.shape, lambda i: (0, 0), pipeline_mode=pl.Buffered(1))


def _proj_res(h, a, w, tm):
    m, d = h.shape
    return pl.pallas_call(
        _proj_res_kernel,
        grid=(m // tm,),
        in_specs=[pl.BlockSpec((tm, d), lambda i: (i, 0)),
                  pl.BlockSpec((a.shape[0], tm, HEAD_DIM), lambda i: (0, i, 0)),
                  _resident_spec(w)],
        out_specs=pl.BlockSpec((tm, d), lambda i: (i, 0)),
        out_shape=jax.ShapeDtypeStruct((m, d), F32),
        compiler_params=_params("arbitrary"),
        name="proj_res",
    )(h, a, w)


def _mlp_kernel(h_ref, ng_ref, w1_ref, w2_ref, o_ref, xn_ref):
    @pl.when(pl.program_id(1) == 0)
    def _():
        x = h_ref[...]
        xn_ref[...] = _rms(x, ng_ref[...]).astype(BF16)
        o_ref[...] = x

    t = jnp.square(jnp.maximum(_dot(xn_ref[...], w1_ref[...]), 0.0)).astype(BF16)
    o_ref[...] += _dot(t, w2_ref[...])


def _mlp(h, norm_g, w1, w2, tm, tf):
    m, d = h.shape
    ff = w1.shape[-1]
    return pl.pallas_call(
        _mlp_kernel,
        grid=(m // tm, ff // tf),
        in_specs=[pl.BlockSpec((tm, d), lambda i, j: (i, 0)),
                  pl.BlockSpec((1, d), lambda i, j: (0, 0)),
                  pl.BlockSpec((d, tf), lambda i, j: (0, j)),
                  pl.BlockSpec((tf, d), lambda i, j: (j, 0))],
        out_specs=pl.BlockSpec((tm, d), lambda i, j: (i, 0)),
        out_shape=jax.ShapeDtypeStruct((m, d), F32),
        scratch_shapes=[pltpu.VMEM((tm, d), BF16)],
        compiler_params=_params("arbitrary", "arbitrary"),
        name="mlp",
    )(h, norm_g.reshape(1, d), w1, w2)


def _norm_proj_kernel(n_normed, block_mean, out_scale, h_ref, ng_ref, hn_ref, w_ref, *outs):
    tm = h_ref.shape[0]
    sub = MOBA_BLOCK if tm % MOBA_BLOCK == 0 else tm
    nh = n_normed // HEAD_DIM
    for r0 in range(0, tm, sub):
        rows = slice(r0, r0 + sub)
        y = _dot(_rms(h_ref[rows, :], ng_ref[...]).astype(BF16), w_ref[...])
        for hh, yh in enumerate(_head_rms(y[:, :n_normed], hn_ref[...])):
            outs[0][hh, rows, :] = (yh if out_scale is None else yh * out_scale).astype(BF16)
            if block_mean:
                outs[-1][0, r0 // MOBA_BLOCK:(r0 + sub) // MOBA_BLOCK, hh * HEAD_DIM:(hh + 1) * HEAD_DIM] = (
                    jnp.sum(yh.reshape(sub // MOBA_BLOCK, MOBA_BLOCK, HEAD_DIM), axis=1) * (1.0 / MOBA_BLOCK))
        for hh in range(y.shape[1] // HEAD_DIM - nh):
            outs[1][hh, rows, :] = y[:, n_normed + hh * HEAD_DIM:n_normed + (hh + 1) * HEAD_DIM].astype(BF16)


def _norm_proj(h, norm_g, head_g, w, tm, n_normed, block_mean, name, out_scale=None):
    m, d = h.shape
    n = w.shape[1]
    nblk = tm // MOBA_BLOCK
    hm = lambda cols: (pl.BlockSpec((cols // HEAD_DIM, tm, HEAD_DIM), lambda i: (0, i, 0)),
                       jax.ShapeDtypeStruct((cols // HEAD_DIM, m, HEAD_DIM), BF16))
    outs = [hm(n_normed)] + ([hm(n - n_normed)] if n > n_normed else [])
    if block_mean:
        outs.append((pl.BlockSpec((1, nblk, n_normed), lambda i: (i, 0, 0)),
                     jax.ShapeDtypeStruct((m // tm, nblk, n_normed), F32)))
    return pl.pallas_call(
        functools.partial(_norm_proj_kernel, n_normed, block_mean, out_scale),
        grid=(m // tm,),
        in_specs=[pl.BlockSpec((tm, d), lambda i: (i, 0)),
                  pl.BlockSpec((1, d), lambda i: (0, 0)),
                  pl.BlockSpec((1, HEAD_DIM), lambda i: (0, 0)),
                  _resident_spec(w)],
        out_specs=[o[0] for o in outs],
        out_shape=[o[1] for o in outs],
        compiler_params=_params("arbitrary"),
        name=name,
    )(h, norm_g.reshape(1, d), head_g.reshape(1, HEAD_DIM), w)


def _moba_kernel(nblk, ncast, q_ref, k_ref, v_ref, km_ref, slope_ref, *refs):
    o_ref, vt_ref, bias_ref = refs[ncast], refs[-2], refs[-1]
    _run_casts(refs[:ncast], refs[ncast + 1:-2])
    for hd in range(q_ref.shape[0]):
        _moba_head(nblk, q_ref.at[hd], k_ref.at[hd], v_ref.at[hd],
                   km_ref[0][:, hd * HEAD_DIM:(hd + 1) * HEAD_DIM], slope_ref[hd], o_ref.at[hd], vt_ref, bias_ref)


def _moba_head(nblk, q_ref, k_ref, v_ref, kmean_f32, slope_row, o_ref, vt_ref, bias_ref):
    blk = MOBA_BLOCK
    grp = blk // SUBLANES
    for n in range(nblk):
        vt_ref[0:HEAD_DIM, n * blk:(n + 1) * blk] = v_ref[n * blk:(n + 1) * blk, :].astype(F32).T.astype(BF16)
    vt_ref[HEAD_DIM:, :] = jnp.ones((vt_ref.shape[0] - HEAD_DIM, vt_ref.shape[1]), BF16)
    slope = slope_row[:, :1]
    kmean = kmean_f32.astype(BF16)
    t_idx = lax.broadcasted_iota(jnp.int32, (blk, blk), 1)
    s_idx = lax.broadcasted_iota(jnp.int32, (blk, blk), 0)
    dist0 = (t_idx - s_idx).astype(F32)
    bias_ref[0] = jnp.where(dist0 >= 0.0, (LOG2E * slope) * dist0, -NEG_BIG)
    for dlt in range(1, nblk):
        bias_ref[dlt] = (LOG2E * slope) * (dist0 + float(dlt * blk))
    n_idx = lax.broadcasted_iota(jnp.int32, (nblk, blk), 0)

    def scores(i):
        qi = q_ref[i * blk:(i + 1) * blk, :]
        sel = None
        if i > MOBA_TOPK:
            gm = jnp.where(n_idx < i, _dot_nt(kmean, qi), NEG_BIG)
            rank = jnp.zeros((nblk, blk), F32)
            for mrow in range(nblk):
                gr = gm[mrow:mrow + 1, :]
                rank = rank + ((gr > gm) | ((gr == gm) & (mrow < n_idx))).astype(F32)
            sel = ((rank < MOBA_TOPK) & (n_idx < i)).astype(F32)
        return sel, _dot_nt(k_ref[0:(i + 1) * blk, :], qi)

    def softmax(i, sel, s_all):
        s = []
        for j in range(i + 1):
            sj = s_all[j * blk:(j + 1) * blk].reshape(grp, SUBLANES, blk) \
                - bias_ref[i - j].reshape(grp, SUBLANES, blk)
            if sel is not None and j < i:
                keep = jnp.broadcast_to(sel[j:j + 1, :], (SUBLANES, blk)) > 0.0
                sj = jnp.where(keep[None], sj, NEG_BIG)
            s.append(sj)
        m8 = functools.reduce(jnp.maximum, [jnp.max(sj, axis=0) for sj in s])
        m = jnp.broadcast_to(jnp.max(m8, axis=0, keepdims=True), (SUBLANES, blk))
        return jnp.concatenate([jnp.exp2(sj - m[None]).reshape(blk, blk).astype(BF16) for sj in s], axis=0)

    order = list(range(nblk - 1, -1, -1))
    depth = 4
    pend = [scores(i) for i in order[:depth]]
    for n, i in enumerate(order):
        cur = pend.pop(0)
        if n + depth < nblk:
            pend.append(scores(order[n + depth]))
        acc = _dot(vt_ref[:, 0:(i + 1) * blk], softmax(i, *cur))
        o_ref[i * blk:(i + 1) * blk, :] = (acc[0:HEAD_DIM] / acc[HEAD_DIM:HEAD_DIM + 1]).T.astype(BF16)


def _moba(q, k, v, kmean, casts, batch, seq):
    heads, m, _ = q.shape
    nblk = seq // MOBA_BLOCK
    hps = HEADS_PER_STEP
    hsteps = heads // hps
    c_in, c_out, c_shape, c_ops = _cast_plan(casts, batch * hsteps, lambda b, h: b * hsteps + h)
    slopes = 2.0 ** (-8.0 * jnp.arange(1, heads + 1, dtype=F32) / heads)
    slopes = jnp.broadcast_to(slopes[:, None, None], (heads, 1, HEAD_DIM))
    spec = pl.BlockSpec((hps, seq, HEAD_DIM), lambda b, h: (h, b, 0))
    outs = pl.pallas_call(
        functools.partial(_moba_kernel, nblk, len(casts)),
        grid=(batch, hsteps),
        in_specs=[spec, spec, spec,
                  pl.BlockSpec((1, nblk, hps * HEAD_DIM), lambda b, h: (b, 0, h)),
                  pl.BlockSpec((hps, 1, HEAD_DIM), lambda b, h: (h, 0, 0))] + c_in,
        out_specs=[spec] + c_out,
        out_shape=[jax.ShapeDtypeStruct((heads, m, HEAD_DIM), BF16)] + c_shape,
        scratch_shapes=[pltpu.VMEM((HEAD_DIM + 2 * SUBLANES, seq), BF16),
                        pltpu.VMEM((nblk, MOBA_BLOCK, MOBA_BLOCK), F32)],
        compiler_params=_params("arbitrary", "arbitrary"),
        name="moba_attn",
    )(q, k, v, kmean, slopes, *c_ops)
    return outs[0], outs[1:]


def _tile(n, pref):
    return pref if n % pref == 0 else n


def kernel(x, a_norm, a_w_in, a_head_norm, a_w_out, lower_bounds, kv_norm, w_kv, k_norm,
           b_norm, b_w_q, b_q_norm, b_w_o, mlp_norm, mlp_w1, mlp_w2):
    batch, seq, d = x.shape
    n_a = a_w_in.shape[0]
    n_b = b_w_q.shape[0]
    assert seq % MOBA_BLOCK == 0 and d % HEAD_DIM == 0
    assert n_a >= 1
    m = batch * seq
    tm = _tile(m, 1024)
    ts = _tile(seq, 2048)
    tr = _tile(m, 512)

    h = x.reshape(m, d)
    kb = vb = kmean = None
    w_in, w_q, w_kvb = a_w_in, None, None
    for l in range(n_a + n_b):
        casts = [(mlp_w1, l), (mlp_w2, l)]
        if l < n_a:
            q, logf, k, v, gate = _hgrn_in(h, a_norm[l], lower_bounds, w_in, l, tm, 256 if w_in.ndim == 3 else 512)
            casts.append((a_w_out, l))
            if l + 1 < n_a:
                casts.append((a_w_in, l + 1))
            elif n_b:
                casts += [(w_kv, 0), (b_w_q, 0)]
            a, wb = _gla(q, k, v, logf, gate, a_head_norm[l], casts, batch, seq, ts)
            if l + 1 < n_a:
                w_in = wb[3]
            elif n_b:
                w_kvb, w_q = wb[3], wb[4]
        else:
            j = l - n_a
            if kb is None:
                kb, vb, kmean = _norm_proj(h, kv_norm, k_norm, w_kvb, tr, d, True, "shared_kv")
                kmean = kmean.reshape(batch, seq // MOBA_BLOCK, d)
            q, = _norm_proj(h, b_norm[j], b_q_norm[j], w_q, tr, d, False, "moba_q",
                            out_scale=LOG2E * HEAD_DIM ** -0.5)
            casts.append((b_w_o, j))
            if j + 1 < n_b:
                casts.append((b_w_q, j + 1))
            a, wb = _moba(q, kb, vb, kmean, casts, batch, seq)
            if j + 1 < n_b:
                w_q = wb[3]
        h = _proj_res(h, a, wb[2], tm)
        h = _mlp(h, mlp_norm[l], wb[0], wb[1], tm, 1024)
    return h.reshape(batch, seq, d)
```

```python
import functools

import numpy as np
import jax
import jax.numpy as jnp
from jax import lax
from jax.experimental import pallas as pl
from jax.experimental.pallas import tpu as pltpu

F32 = jnp.float32
BF16 = jnp.bfloat16

HEAD_DIM = 128
MOBA_BLOCK = 256
MOBA_TOPK = 3
GLA_CHUNK = 64
EPS = 1e-6
NEG_BIG = -1e30
LB_FLOOR = 1e-30
LOG2E = 1.4426950408889634
SUBLANES = 8
HEADS_PER_STEP = 2

V7X_VMEM_LIMIT_BYTES = 60 * 1024 * 1024


def _params(*sem):
    return pltpu.CompilerParams(dimension_semantics=sem, vmem_limit_bytes=V7X_VMEM_LIMIT_BYTES)


def _dot(a, b):
    return jnp.dot(a, b, preferred_element_type=F32)


def _dot_nt(a, b):
    return lax.dot_general(a, b, (((1,), (1,)), ((), ())), preferred_element_type=F32)


def _dot_tn(a, b):
    return lax.dot_general(a, b, (((0,), (0,)), ((), ())), preferred_element_type=F32)


def _cast_plan(casts, steps, step_index):
    in_specs, out_specs, out_shapes, operands = [], [], [], []
    for w, layer in casts:
        if w.ndim == 2:
            w, layer = w.reshape((1,) + w.shape), 0
        rows, cols = w.shape[1] // steps, w.shape[2]
        in_specs.append(pl.BlockSpec((None, rows, cols), lambda *g, layer=layer: (layer, step_index(*g), 0)))
        out_specs.append(pl.BlockSpec((rows, cols), lambda *g: (step_index(*g), 0)))
        out_shapes.append(jax.ShapeDtypeStruct(w.shape[1:], BF16))
        operands.append(w)
    return in_specs, out_specs, out_shapes, operands


def _run_casts(in_refs, out_refs):
    for src, dst in zip(in_refs, out_refs):
        dst[...] = src[...].astype(BF16)


def _store_heads(ref, x):
    for hh in range(ref.shape[0]):
        ref[hh] = x[:, hh * HEAD_DIM:(hh + 1) * HEAD_DIM]


def _load_heads(ref):
    return jnp.concatenate([ref[hh] for hh in range(ref.shape[0])], axis=1)


def _rms(x, g):
    return x * lax.rsqrt(jnp.mean(x * x, axis=-1, keepdims=True) + EPS) * g


def _silu(x):
    return x * jax.nn.sigmoid(x)


def _head_rms(x, g):
    outs = []
    for hh in range(x.shape[1] // HEAD_DIM):
        outs.append(_rms(x[:, hh * HEAD_DIM:(hh + 1) * HEAD_DIM], g))
    return outs


W_RING = 3


def _hgrn_in_kernel(layer, stacked, h_ref, ng_ref, lb_ref, w_hbm, q_ref, lf_ref, k_ref, v_ref, gt_ref,
                    xn_ref, wbuf, sem):
    nj = pl.num_programs(1)
    steps = pl.num_programs(0) * nj
    s = pl.program_id(0) * nj + pl.program_id(1)
    tn = wbuf.shape[-1]

    def copies(step):
        slot, col = step % W_RING, step % nj
        out = []
        for g in range(wbuf.shape[1]):
            cols = pl.ds(pl.multiple_of((col + g * nj) * tn, tn), tn)
            src = w_hbm.at[layer, :, cols] if stacked else w_hbm.at[:, cols]
            out.append(pltpu.make_async_copy(src, wbuf.at[slot, g], sem.at[slot, g]))
        return out

    @pl.when(s == 0)
    def _():
        for ahead in range(W_RING - 1):
            for cp in copies(ahead):
                cp.start()

    @pl.when(s + (W_RING - 1) < steps)
    def _():
        for cp in copies(s + (W_RING - 1)):
            cp.start()

    @pl.when(pl.program_id(1) == 0)
    def _():
        xn_ref[...] = _rms(h_ref[...], ng_ref[...]).astype(BF16)

    for cp in copies(s):
        cp.wait()
    slot = s % W_RING
    wdot = lambda g: jnp.dot(xn, wbuf[slot, g].astype(BF16), preferred_element_type=F32)

    xn = xn_ref[...]
    pf = wdot(1)
    pq = wdot(0)
    pg = wdot(3)
    pv = wdot(2)

    lbr = lb_ref[...]
    rows = [lbr[r:r + 1, :] for r in range(lbr.shape[0])]
    mx = functools.reduce(jnp.maximum, rows)
    ex = [jnp.exp(r - mx) for r in rows]
    den = functools.reduce(lambda a, b: a + b, ex)
    p = [e / den for e in ex]
    lb = functools.reduce(lambda a, b: a + b, p[:layer + 1]) - p[0]

    e = jnp.exp(-jnp.abs(pf))
    r = 1.0 / (1.0 + e)
    er = e * r
    pos = pf >= 0.0
    one_m = 1.0 - lb
    _store_heads(lf_ref, jnp.log(jnp.maximum(lb, LB_FLOOR) + one_m * jnp.where(pos, r, er)))
    _store_heads(k_ref, (one_m * jnp.where(pos, er, r)).astype(BF16))
    _store_heads(q_ref, _silu(pq).astype(BF16))
    _store_heads(gt_ref, _silu(pg).astype(BF16))
    _store_heads(v_ref, pv.astype(BF16))


def _hgrn_in(h, norm_g, lower_bounds, w_in, layer, tm, tn):
    m, d = h.shape
    nj = d // tn
    groups = w_in.shape[-1] // d
    assert (m // tm) * nj >= W_RING - 1
    heads = d // HEAD_DIM
    ospec = pl.BlockSpec((tn // HEAD_DIM, tm, HEAD_DIM), lambda i, j: (j, i, 0))
    return pl.pallas_call(
        functools.partial(_hgrn_in_kernel, layer, w_in.ndim == 3),
        grid=(m // tm, nj),
        in_specs=[pl.BlockSpec((tm, d), lambda i, j: (i, 0)),
                  pl.BlockSpec((1, d), lambda i, j: (0, 0)),
                  pl.BlockSpec((lower_bounds.shape[0], tn), lambda i, j: (0, j)),
                  pl.BlockSpec(memory_space=pl.ANY)],
        out_specs=[ospec] * 5,
        out_shape=[jax.ShapeDtypeStruct((heads, m, HEAD_DIM), dt) for dt in (BF16, F32, BF16, BF16, BF16)],
        scratch_shapes=[pltpu.VMEM((tm, d), BF16),
                        pltpu.VMEM((W_RING, groups, d, tn), w_in.dtype),
                        pltpu.SemaphoreType.DMA((W_RING, groups))],
        compiler_params=_params("arbitrary", "arbitrary"),
        name="hgrn_in",
    )(h, norm_g.reshape(1, d), lower_bounds, w_in)


def _gla_tables(c):
    r = np.arange(c)
    j = r[None, :]
    mats = [j <= r[:, None]]
    small = (4, 2, 1)
    for m in small:
        seg, pos = r // (2 * m), r % (2 * m)
        mid = (seg * 2 * m + m)[:, None]
        second = (pos >= m)[:, None]
        mats.append(np.where(second, (j >= mid) & (j <= r[:, None]), (j > r[:, None]) & (j <= mid - 1)))
    d = np.concatenate(mats, 0).astype(np.float32)
    dmat = np.concatenate([d, d, d], axis=1)
    lev = np.full((c, c), -1, np.int32)
    for li, m in enumerate(small):
        seg, pos = r // (2 * m), r % (2 * m)
        ok = (seg[:, None] == seg[None, :]) & (pos >= m)[:, None] & (pos < m)[None, :]
        lev[ok] = li
    big = []
    m = c // 2
    while m >= SUBLANES:
        t = np.concatenate([np.arange(s0 + m, s0 + 2 * m) for s0 in range(0, c, 2 * m)])
        ok = (t[:, None] // (2 * m) == r[None, :] // (2 * m)) & ((r % (2 * m)) < m)[None, :]
        big.append(ok.astype(np.float32))
        m //= 2
    return dmat, lev, np.stack(big)


def _gla_kernel(chunk, ncast, q_ref, k_ref, v_ref, g_ref, gt_ref, hg_ref, dmat_ref, lev_ref, big_ref, *refs):
    o_ref, st_ref = refs[ncast], refs[-1]

    @pl.when(pl.program_id(2) == 0)
    def _():
        st_ref[...] = jnp.zeros_like(st_ref)

    _run_casts(refs[:ncast], refs[ncast + 1:-1])
    for hd in range(q_ref.shape[0]):
        _gla_head(chunk, q_ref.at[hd], k_ref.at[hd], v_ref.at[hd], g_ref.at[hd], gt_ref.at[hd],
                  hg_ref[:, hd * HEAD_DIM:(hd + 1) * HEAD_DIM], dmat_ref, lev_ref, big_ref, o_ref.at[hd], st_ref.at[hd])


def _gla_head(chunk, q_ref, k_ref, v_ref, g_ref, gt_ref, hg, dmat_ref, lev_ref, big_ref, o_ref, st_ref):
    c = chunk
    nc = q_ref.shape[0] // c
    lev = lev_ref[...]

    g = g_ref[...] * LOG2E
    g_hi = g.astype(BF16)
    r1 = g - g_hi.astype(F32)
    g_mid = r1.astype(BF16)
    g_lo = (r1 - g_mid.astype(F32)).astype(BF16)
    lanes = lambda x: jnp.concatenate([x[i * c:(i + 1) * c] for i in range(nc)], axis=1)
    e_all = _dot(dmat_ref[...], jnp.concatenate([lanes(g_hi), lanes(g_mid), lanes(g_lo)], axis=0))

    small_p, big_p, big_tgt = [], [], []
    for ci in range(nc):
        sl = pl.ds(ci * c, c)
        cols = slice(ci * HEAD_DIM, (ci + 1) * HEAD_DIM)
        b = e_all[0:c, cols]
        q = q_ref[sl, :].astype(F32)
        k = k_ref[sl, :].astype(F32)
        ps = []
        for li in range(3):
            w = jnp.exp2(e_all[(li + 1) * c:(li + 2) * c, cols])
            ps.append(_dot_nt((q * w).astype(BF16), (k * w).astype(BF16)))
        small_p.append(ps)
        ps, tg = [], []
        m = c // 2
        while m >= SUBLANES:
            qs, ks, tgt = [], [], []
            for s0 in range(0, c, 2 * m):
                ref = b[s0 + m - 1:s0 + m, :]
                qs.append(q[s0 + m:s0 + 2 * m] * jnp.exp2(b[s0 + m:s0 + 2 * m] - ref))
                ks.append(k[s0:s0 + m] * jnp.exp2(ref - b[s0:s0 + m]))
                ks.append(k[s0 + m:s0 + 2 * m])
                tgt.extend(range((s0 + m) // SUBLANES, (s0 + 2 * m) // SUBLANES))
            ps.append(_dot_nt(jnp.concatenate(qs, 0).astype(BF16), jnp.concatenate(ks, 0).astype(BF16)))
            tg.append(tgt)
            m //= 2
        big_p.append(ps)
        big_tgt.append(tg)

    intra, qbs, upds, dcols = [], [], [], []
    for ci in range(nc):
        sl = pl.ds(ci * c, c)
        cols = slice(ci * HEAD_DIM, (ci + 1) * HEAD_DIM)
        b = e_all[0:c, cols]
        q = q_ref[sl, :].astype(F32)
        k = k_ref[sl, :].astype(F32)
        v = v_ref[sl, :]
        rows = [jnp.zeros((SUBLANES, c), F32) for _ in range(c // SUBLANES)]
        for li, p in enumerate(small_p[ci]):
            p = jnp.where(lev == li, p, 0.0)
            rows = [rw + p[i * SUBLANES:(i + 1) * SUBLANES] for i, rw in enumerate(rows)]
        for li, (p, tgt) in enumerate(zip(big_p[ci], big_tgt[ci])):
            p = p * big_ref[li]
            for n, i in enumerate(tgt):
                rows[i] = rows[i] + p[n * SUBLANES:(n + 1) * SUBLANES]
        scores = jnp.concatenate(rows, 0).astype(BF16)
        wb = jnp.exp2(b)
        we = jnp.exp2(b[c - 1:c, :] - b)
        intra.append(_dot(scores, v) + jnp.sum(q * k, axis=-1, keepdims=True) * v.astype(F32))
        qbs.append((q * wb).astype(BF16))
        upds.append(_dot_tn((k * we).astype(BF16), v))
        dcols.append(jnp.broadcast_to(wb[c - 1:c, :], (SUBLANES, HEAD_DIM)).T[:, :1])

    st = st_ref[...]
    states = []
    for ci in range(nc):
        states.append(st.astype(BF16))
        st = st * dcols[ci] + upds[ci]
    st_ref[...] = st

    for ci in range(nc):
        sl = pl.ds(ci * c, c)
        o = _dot(qbs[ci], states[ci]) + intra[ci]
        o_ref[sl, :] = (_rms(o, hg) * gt_ref[sl, :].astype(F32)).astype(BF16)


def _gla(q, k, v, logf, gate, head_gain, casts, batch, seq, ts):
    heads, m, _ = q.shape
    d = heads * HEAD_DIM
    ns = seq // ts
    dmat, lev, big = _gla_tables(GLA_CHUNK)
    hps = HEADS_PER_STEP
    hsteps = heads // hps
    spec = pl.BlockSpec((hps, ts, HEAD_DIM), lambda b, h, s: (h, b * ns + s, 0))
    const = lambda a: pl.BlockSpec(a.shape, lambda b, h, s: (0,) * a.ndim)
    c_in, c_out, c_shape, c_ops = _cast_plan(casts, batch * hsteps * ns, lambda b, h, s: (b * hsteps + h) * ns + s)
    outs = pl.pallas_call(
        functools.partial(_gla_kernel, GLA_CHUNK, len(casts)),
        grid=(batch, hsteps, ns),
        in_specs=[spec, spec, spec, spec, spec,
                  pl.BlockSpec((1, hps * HEAD_DIM), lambda b, h, s: (0, h)),
                  const(dmat), const(lev), const(big)] + c_in,
        out_specs=[spec] + c_out,
        out_shape=[jax.ShapeDtypeStruct((heads, m, HEAD_DIM), BF16)] + c_shape,
        scratch_shapes=[pltpu.VMEM((hps, HEAD_DIM, HEAD_DIM), F32)],
        compiler_params=_params("arbitrary", "arbitrary", "arbitrary"),
        name="gla",
    )(q, k, v, logf, gate, head_gain.reshape(1, d), jnp.asarray(dmat, BF16), jnp.asarray(lev),
      jnp.asarray(big), *c_ops)
    return outs[0], outs[1:]


def _proj_res_kernel(h_ref, a_ref, w_ref, o_ref):
    o_ref[...] = h_ref[...] + _dot(_load_heads(a_ref), w_ref[...])


def _resident_spec(w):
    return pl.BlockSpec(w.shape, lambda i: (0, 0), pipeline_mode=pl.Buffered(1))


def _proj_res(h, a, w, tm):
    m, d = h.shape
    return pl.pallas_call(
        _proj_res_kernel,
        grid=(m // tm,),
        in_specs=[pl.BlockSpec((tm, d), lambda i: (i, 0)),
                  pl.BlockSpec((a.shape[0], tm, HEAD_DIM), lambda i: (0, i, 0)),
                  _resident_spec(w)],
        out_specs=pl.BlockSpec((tm, d), lambda i: (i, 0)),
        out_shape=jax.ShapeDtypeStruct((m, d), F32),
        compiler_params=_params("arbitrary"),
        name="proj_res",
    )(h, a, w)


def _mlp_kernel(h_ref, ng_ref, w1_ref, w2_ref, o_ref, xn_ref):
    @pl.when(pl.program_id(1) == 0)
    def _():
        x = h_ref[...]
        xn_ref[...] = _rms(x, ng_ref[...]).astype(BF16)
        o_ref[...] = x

    t = jnp.square(jnp.maximum(_dot(xn_ref[...], w1_ref[...]), 0.0)).astype(BF16)
    o_ref[...] += _dot(t, w2_ref[...])


def _mlp(h, norm_g, w1, w2, tm, tf):
    m, d = h.shape
    ff = w1.shape[-1]
    return pl.pallas_call(
        _mlp_kernel,
        grid=(m // tm, ff // tf),
        in_specs=[pl.BlockSpec((tm, d), lambda i, j: (i, 0)),
                  pl.BlockSpec((1, d), lambda i, j: (0, 0)),
                  pl.BlockSpec((d, tf), lambda i, j: (0, j)),
                  pl.BlockSpec((tf, d), lambda i, j: (j, 0))],
        out_specs=pl.BlockSpec((tm, d), lambda i, j: (i, 0)),
        out_shape=jax.ShapeDtypeStruct((m, d), F32),
        scratch_shapes=[pltpu.VMEM((tm, d), BF16)],
        compiler_params=_params("arbitrary", "arbitrary"),
        name="mlp",
    )(h, norm_g.reshape(1, d), w1, w2)


def _norm_proj_kernel(n_normed, block_mean, out_scale, h_ref, ng_ref, hn_ref, w_ref, *outs):
    tm = h_ref.shape[0]
    sub = MOBA_BLOCK if tm % MOBA_BLOCK == 0 else tm
    nh = n_normed // HEAD_DIM
    for r0 in range(0, tm, sub):
        rows = slice(r0, r0 + sub)
        y = _dot(_rms(h_ref[rows, :], ng_ref[...]).astype(BF16), w_ref[...])
        for hh, yh in enumerate(_head_rms(y[:, :n_normed], hn_ref[...])):
            outs[0][hh, rows, :] = (yh if out_scale is None else yh * out_scale).astype(BF16)
            if block_mean:
                outs[-1][0, r0 // MOBA_BLOCK:(r0 + sub) // MOBA_BLOCK, hh * HEAD_DIM:(hh + 1) * HEAD_DIM] = (
                    jnp.sum(yh.reshape(sub // MOBA_BLOCK, MOBA_BLOCK, HEAD_DIM), axis=1) * (1.0 / MOBA_BLOCK))
        for hh in range(y.shape[1] // HEAD_DIM - nh):
            outs[1][hh, rows, :] = y[:, n_normed + hh * HEAD_DIM:n_normed + (hh + 1) * HEAD_DIM].astype(BF16)


def _norm_proj(h, norm_g, head_g, w, tm, n_normed, block_mean, name, out_scale=None):
    m, d = h.shape
    n = w.shape[1]
    nblk = tm // MOBA_BLOCK
    hm = lambda cols: (pl.BlockSpec((cols // HEAD_DIM, tm, HEAD_DIM), lambda i: (0, i, 0)),
                       jax.ShapeDtypeStruct((cols // HEAD_DIM, m, HEAD_DIM), BF16))
    outs = [hm(n_normed)] + ([hm(n - n_normed)] if n > n_normed else [])
    if block_mean:
        outs.append((pl.BlockSpec((1, nblk, n_normed), lambda i: (i, 0, 0)),
                     jax.ShapeDtypeStruct((m // tm, nblk, n_normed), F32)))
    return pl.pallas_call(
        functools.partial(_norm_proj_kernel, n_normed, block_mean, out_scale),
        grid=(m // tm,),
        in_specs=[pl.BlockSpec((tm, d), lambda i: (i, 0)),
                  pl.BlockSpec((1, d), lambda i: (0, 0)),
                  pl.BlockSpec((1, HEAD_DIM), lambda i: (0, 0)),
                  _resident_spec(w)],
        out_specs=[o[0] for o in outs],
        out_shape=[o[1] for o in outs],
        compiler_params=_params("arbitrary"),
        name=name,
    )(h, norm_g.reshape(1, d), head_g.reshape(1, HEAD_DIM), w)


def _moba_kernel(nblk, ncast, q_ref, k_ref, v_ref, km_ref, slope_ref, *refs):
    o_ref, vt_ref, bias_ref = refs[ncast], refs[-2], refs[-1]
    _run_casts(refs[:ncast], refs[ncast + 1:-2])
    for hd in range(q_ref.shape[0]):
        _moba_head(nblk, q_ref.at[hd], k_ref.at[hd], v_ref.at[hd],
                   km_ref[0][:, hd * HEAD_DIM:(hd + 1) * HEAD_DIM], slope_ref[hd], o_ref.at[hd], vt_ref, bias_ref)


def _moba_head(nblk, q_ref, k_ref, v_ref, kmean_f32, slope_row, o_ref, vt_ref, bias_ref):
    blk = MOBA_BLOCK
    grp = blk // SUBLANES
    for n in range(nblk):
        vt_ref[0:HEAD_DIM, n * blk:(n + 1) * blk] = v_ref[n * blk:(n + 1) * blk, :].astype(F32).T.astype(BF16)
    vt_ref[HEAD_DIM:, :] = jnp.ones((vt_ref.shape[0] - HEAD_DIM, vt_ref.shape[1]), BF16)
    slope = slope_row[:, :1]
    kmean = kmean_f32.astype(BF16)
    t_idx = lax.broadcasted_iota(jnp.int32, (blk, blk), 1)
    s_idx = lax.broadcasted_iota(jnp.int32, (blk, blk), 0)
    dist0 = (t_idx - s_idx).astype(F32)
    bias_ref[0] = jnp.where(dist0 >= 0.0, (LOG2E * slope) * dist0, -NEG_BIG)
    for dlt in range(1, nblk):
        bias_ref[dlt] = (LOG2E * slope) * (dist0 + float(dlt * blk))
    n_idx = lax.broadcasted_iota(jnp.int32, (nblk, blk), 0)

    def scores(i):
        qi = q_ref[i * blk:(i + 1) * blk, :]
        sel = None
        if i > MOBA_TOPK:
            gm = jnp.where(n_idx < i, _dot_nt(kmean, qi), NEG_BIG)
            rank = jnp.zeros((nblk, blk), F32)
            for mrow in range(nblk):
                gr = gm[mrow:mrow + 1, :]
                rank = rank + ((gr > gm) | ((gr == gm) & (mrow < n_idx))).astype(F32)
            sel = ((rank < MOBA_TOPK) & (n_idx < i)).astype(F32)
        return sel, _dot_nt(k_ref[0:(i + 1) * blk, :], qi)

    def softmax(i, sel, s_all):
        s = []
        for j in range(i + 1):
            sj = s_all[j * blk:(j + 1) * blk].reshape(grp, SUBLANES, blk) \
                - bias_ref[i - j].reshape(grp, SUBLANES, blk)
            if sel is not None and j < i:
                keep = jnp.broadcast_to(sel[j:j + 1, :], (SUBLANES, blk)) > 0.0
                sj = jnp.where(keep[None], sj, NEG_BIG)
            s.append(sj)
        m8 = functools.reduce(jnp.maximum, [jnp.max(sj, axis=0) for sj in s])
        m = jnp.broadcast_to(jnp.max(m8, axis=0, keepdims=True), (SUBLANES, blk))
        return jnp.concatenate([jnp.exp2(sj - m[None]).reshape(blk, blk).astype(BF16) for sj in s], axis=0)

    order = list(range(nblk - 1, -1, -1))
    depth = 4
    pend = [scores(i) for i in order[:depth]]
    for n, i in enumerate(order):
        cur = pend.pop(0)
        if n + depth < nblk:
            pend.append(scores(order[n + depth]))
        acc = _dot(vt_ref[:, 0:(i + 1) * blk], softmax(i, *cur))
        o_ref[i * blk:(i + 1) * blk, :] = (acc[0:HEAD_DIM] / acc[HEAD_DIM:HEAD_DIM + 1]).T.astype(BF16)


def _moba(q, k, v, kmean, casts, batch, seq):
    heads, m, _ = q.shape
    nblk = seq // MOBA_BLOCK
    hps = HEADS_PER_STEP
    hsteps = heads // hps
    c_in, c_out, c_shape, c_ops = _cast_plan(casts, batch * hsteps, lambda b, h: b * hsteps + h)
    slopes = 2.0 ** (-8.0 * jnp.arange(1, heads + 1, dtype=F32) / heads)
    slopes = jnp.broadcast_to(slopes[:, None, None], (heads, 1, HEAD_DIM))
    spec = pl.BlockSpec((hps, seq, HEAD_DIM), lambda b, h: (h, b, 0))
    outs = pl.pallas_call(
        functools.partial(_moba_kernel, nblk, len(casts)),
        grid=(batch, hsteps),
        in_specs=[spec, spec, spec,
                  pl.BlockSpec((1, nblk, hps * HEAD_DIM), lambda b, h: (b, 0, h)),
                  pl.BlockSpec((hps, 1, HEAD_DIM), lambda b, h: (h, 0, 0))] + c_in,
        out_specs=[spec] + c_out,
        out_shape=[jax.ShapeDtypeStruct((heads, m, HEAD_DIM), BF16)] + c_shape,
        scratch_shapes=[pltpu.VMEM((HEAD_DIM + 2 * SUBLANES, seq), BF16),
                        pltpu.VMEM((nblk, MOBA_BLOCK, MOBA_BLOCK), F32)],
        compiler_params=_params("arbitrary", "arbitrary"),
        name="moba_attn",
    )(q, k, v, kmean, slopes, *c_ops)
    return outs[0], outs[1:]


def _tile(n, pref):
    return pref if n % pref == 0 else n


def kernel(x, a_norm, a_w_in, a_head_norm, a_w_out, lower_bounds, kv_norm, w_kv, k_norm,
           b_norm, b_w_q, b_q_norm, b_w_o, mlp_norm, mlp_w1, mlp_w2):
    batch, seq, d = x.shape
    n_a = a_w_in.shape[0]
    n_b = b_w_q.shape[0]
    assert seq % MOBA_BLOCK == 0 and d % HEAD_DIM == 0
    assert n_a >= 1
    m = batch * seq
    tm = _tile(m, 1024)
    ts = _tile(seq, 2048)
    tr = _tile(m, 512)

    h = x.reshape(m, d)
    kb = vb = kmean = None
    w_in, w_q, w_kvb = a_w_in, None, None
    for l in range(n_a + n_b):
        casts = [(mlp_w1, l), (mlp_w2, l)]
        if l < n_a:
            q, logf, k, v, gate = _hgrn_in(h, a_norm[l], lower_bounds, w_in, l, tm, 256)
            casts.append((a_w_out, l))
            if l + 1 < n_a:
                casts.append((a_w_in, l + 1))
            elif n_b:
                casts += [(w_kv, 0), (b_w_q, 0)]
            a, wb = _gla(q, k, v, logf, gate, a_head_norm[l], casts, batch, seq, ts)
            if l + 1 < n_a:
                w_in = wb[3]
            elif n_b:
                w_kvb, w_q = wb[3], wb[4]
        else:
            j = l - n_a
            if kb is None:
                kb, vb, kmean = _norm_proj(h, kv_norm, k_norm, w_kvb, tr, d, True, "shared_kv")
                kmean = kmean.reshape(batch, seq // MOBA_BLOCK, d)
            q, = _norm_proj(h, b_norm[j], b_q_norm[j], w_q, tr, d, False, "moba_q",
                            out_scale=LOG2E * HEAD_DIM ** -0.5)
            casts.append((b_w_o, j))
            if j + 1 < n_b:
                casts.append((b_w_q, j + 1))
            a, wb = _moba(q, kb, vb, kmean, casts, batch, seq)
            if j + 1 < n_b:
                w_q = wb[3]
        h = _proj_res(h, a, wb[2], tm)
        h = _mlp(h, mlp_norm[l], wb[0], wb[1], tm, 1024)
    return h.reshape(batch, seq, d)
```

```python
import functools

import numpy as np
import jax
import jax.numpy as jnp
from jax import lax
from jax.experimental import pallas as pl
from jax.experimental.pallas import tpu as pltpu

F32 = jnp.float32
BF16 = jnp.bfloat16

HEAD_DIM = 128
MOBA_BLOCK = 256
MOBA_TOPK = 3
GLA_CHUNK = 64
EPS = 1e-6
NEG_BIG = -1e30
LB_FLOOR = 1e-30
LOG2E = 1.4426950408889634
SUBLANES = 8
HEADS_PER_STEP = 2

V7X_VMEM_LIMIT_BYTES = 60 * 1024 * 1024


def _params(*sem):
    return pltpu.CompilerParams(dimension_semantics=sem, vmem_limit_bytes=V7X_VMEM_LIMIT_BYTES)


def _dot(a, b):
    return jnp.dot(a, b, preferred_element_type=F32)


def _dot_nt(a, b):
    return lax.dot_general(a, b, (((1,), (1,)), ((), ())), preferred_element_type=F32)


def _dot_tn(a, b):
    return lax.dot_general(a, b, (((0,), (0,)), ((), ())), preferred_element_type=F32)


def _cast_plan(casts, steps, step_index):
    in_specs, out_specs, out_shapes, operands = [], [], [], []
    for w, layer in casts:
        if w.ndim == 2:
            w, layer = w.reshape((1,) + w.shape), 0
        rows, cols = w.shape[1] // steps, w.shape[2]
        in_specs.append(pl.BlockSpec((None, rows, cols), lambda *g, layer=layer: (layer, step_index(*g), 0)))
        out_specs.append(pl.BlockSpec((rows, cols), lambda *g: (step_index(*g), 0)))
        out_shapes.append(jax.ShapeDtypeStruct(w.shape[1:], BF16))
        operands.append(w)
    return in_specs, out_specs, out_shapes, operands


def _run_casts(in_refs, out_refs):
    for src, dst in zip(in_refs, out_refs):
        dst[...] = src[...].astype(BF16)


def _store_heads(ref, x):
    for hh in range(ref.shape[0]):
        ref[hh] = x[:, hh * HEAD_DIM:(hh + 1) * HEAD_DIM]


def _load_heads(ref):
    return jnp.concatenate([ref[hh] for hh in range(ref.shape[0])], axis=1)


def _rms(x, g):
    return x * lax.rsqrt(jnp.mean(x * x, axis=-1, keepdims=True) + EPS) * g


def _silu(x):
    return x * jax.nn.sigmoid(x)


def _head_rms(x, g):
    outs = []
    for hh in range(x.shape[1] // HEAD_DIM):
        outs.append(_rms(x[:, hh * HEAD_DIM:(hh + 1) * HEAD_DIM], g))
    return outs


W_RING = 3


def _hgrn_in_kernel(layer, stacked, h_ref, ng_ref, lb_ref, w_hbm, q_ref, lf_ref, k_ref, v_ref, gt_ref,
                    xn_ref, wbuf, sem):
    nj = pl.num_programs(1)
    steps = pl.num_programs(0) * nj
    s = pl.program_id(0) * nj + pl.program_id(1)
    tn = wbuf.shape[-1]

    def copies(step):
        slot, col = step % W_RING, step % nj
        out = []
        for g in range(wbuf.shape[1]):
            cols = pl.ds(pl.multiple_of((col + g * nj) * tn, tn), tn)
            src = w_hbm.at[layer, :, cols] if stacked else w_hbm.at[:, cols]
            out.append(pltpu.make_async_copy(src, wbuf.at[slot, g], sem.at[slot, g]))
        return out

    @pl.when(s == 0)
    def _():
        for ahead in range(W_RING - 1):
            for g, cp in enumerate(copies(ahead)):
                cp.start(priority=g % 2)

    @pl.when(s + (W_RING - 1) < steps)
    def _():
        for g, cp in enumerate(copies(s + (W_RING - 1))):
            cp.start(priority=g % 2)

    @pl.when(pl.program_id(1) == 0)
    def _():
        xn_ref[...] = _rms(h_ref[...], ng_ref[...]).astype(BF16)

    for cp in copies(s):
        cp.wait()
    slot = s % W_RING
    wdot = lambda g: jnp.dot(xn, wbuf[slot, g].astype(BF16), preferred_element_type=F32)

    xn = xn_ref[...]
    pf = wdot(1)
    pq = wdot(0)
    pg = wdot(3)
    pv = wdot(2)

    lbr = lb_ref[...]
    rows = [lbr[r:r + 1, :] for r in range(lbr.shape[0])]
    mx = functools.reduce(jnp.maximum, rows)
    ex = [jnp.exp(r - mx) for r in rows]
    den = functools.reduce(lambda a, b: a + b, ex)
    p = [e / den for e in ex]
    lb = functools.reduce(lambda a, b: a + b, p[:layer + 1]) - p[0]

    e = jnp.exp(-jnp.abs(pf))
    r = 1.0 / (1.0 + e)
    er = e * r
    pos = pf >= 0.0
    one_m = 1.0 - lb
    _store_heads(lf_ref, jnp.log(jnp.maximum(lb, LB_FLOOR) + one_m * jnp.where(pos, r, er)))
    _store_heads(k_ref, (one_m * jnp.where(pos, er, r)).astype(BF16))
    _store_heads(q_ref, _silu(pq).astype(BF16))
    _store_heads(gt_ref, _silu(pg).astype(BF16))
    _store_heads(v_ref, pv.astype(BF16))


def _hgrn_in(h, norm_g, lower_bounds, w_in, layer, tm, tn):
    m, d = h.shape
    nj = d // tn
    groups = w_in.shape[-1] // d
    assert (m // tm) * nj >= W_RING - 1
    heads = d // HEAD_DIM
    ospec = pl.BlockSpec((tn // HEAD_DIM, tm, HEAD_DIM), lambda i, j: (j, i, 0))
    return pl.pallas_call(
        functools.partial(_hgrn_in_kernel, layer, w_in.ndim == 3),
        grid=(m // tm, nj),
        in_specs=[pl.BlockSpec((tm, d), lambda i, j: (i, 0)),
                  pl.BlockSpec((1, d), lambda i, j: (0, 0)),
                  pl.BlockSpec((lower_bounds.shape[0], tn), lambda i, j: (0, j)),
                  pl.BlockSpec(memory_space=pl.ANY)],
        out_specs=[ospec] * 5,
        out_shape=[jax.ShapeDtypeStruct((heads, m, HEAD_DIM), dt) for dt in (BF16, F32, BF16, BF16, BF16)],
        scratch_shapes=[pltpu.VMEM((tm, d), BF16),
                        pltpu.VMEM((W_RING, groups, d, tn), w_in.dtype),
                        pltpu.SemaphoreType.DMA((W_RING, groups))],
        compiler_params=_params("arbitrary", "arbitrary"),
        name="hgrn_in",
    )(h, norm_g.reshape(1, d), lower_bounds, w_in)


def _gla_tables(c):
    r = np.arange(c)
    j = r[None, :]
    mats = [j <= r[:, None]]
    small = (4, 2, 1)
    for m in small:
        seg, pos = r // (2 * m), r % (2 * m)
        mid = (seg * 2 * m + m)[:, None]
        second = (pos >= m)[:, None]
        mats.append(np.where(second, (j >= mid) & (j <= r[:, None]), (j > r[:, None]) & (j <= mid - 1)))
    d = np.concatenate(mats, 0).astype(np.float32)
    dmat = np.concatenate([d, d, d], axis=1)
    lev = np.full((c, c), -1, np.int32)
    for li, m in enumerate(small):
        seg, pos = r // (2 * m), r % (2 * m)
        ok = (seg[:, None] == seg[None, :]) & (pos >= m)[:, None] & (pos < m)[None, :]
        lev[ok] = li
    big = []
    m = c // 2
    while m >= SUBLANES:
        t = np.concatenate([np.arange(s0 + m, s0 + 2 * m) for s0 in range(0, c, 2 * m)])
        ok = (t[:, None] // (2 * m) == r[None, :] // (2 * m)) & ((r % (2 * m)) < m)[None, :]
        big.append(ok.astype(np.float32))
        m //= 2
    return dmat, lev, np.stack(big)


def _gla_kernel(chunk, ncast, q_ref, k_ref, v_ref, g_ref, gt_ref, hg_ref, dmat_ref, lev_ref, big_ref, *refs):
    o_ref, st_ref = refs[ncast], refs[-1]

    @pl.when(pl.program_id(2) == 0)
    def _():
        st_ref[...] = jnp.zeros_like(st_ref)

    _run_casts(refs[:ncast], refs[ncast + 1:-1])
    for hd in range(q_ref.shape[0]):
        _gla_head(chunk, q_ref.at[hd], k_ref.at[hd], v_ref.at[hd], g_ref.at[hd], gt_ref.at[hd],
                  hg_ref[:, hd * HEAD_DIM:(hd + 1) * HEAD_DIM], dmat_ref, lev_ref, big_ref, o_ref.at[hd], st_ref.at[hd])


def _gla_head(chunk, q_ref, k_ref, v_ref, g_ref, gt_ref, hg, dmat_ref, lev_ref, big_ref, o_ref, st_ref):
    c = chunk
    nc = q_ref.shape[0] // c
    lev = lev_ref[...]

    g = g_ref[...] * LOG2E
    g_hi = g.astype(BF16)
    r1 = g - g_hi.astype(F32)
    g_mid = r1.astype(BF16)
    g_lo = (r1 - g_mid.astype(F32)).astype(BF16)
    lanes = lambda x: jnp.concatenate([x[i * c:(i + 1) * c] for i in range(nc)], axis=1)
    e_all = _dot(dmat_ref[...], jnp.concatenate([lanes(g_hi), lanes(g_mid), lanes(g_lo)], axis=0))

    small_p, big_p, big_tgt = [], [], []
    for ci in range(nc):
        sl = pl.ds(ci * c, c)
        cols = slice(ci * HEAD_DIM, (ci + 1) * HEAD_DIM)
        b = e_all[0:c, cols]
        q = q_ref[sl, :].astype(F32)
        k = k_ref[sl, :].astype(F32)
        ps = []
        for li in range(3):
            w = jnp.exp2(e_all[(li + 1) * c:(li + 2) * c, cols])
            ps.append(_dot_nt((q * w).astype(BF16), (k * w).astype(BF16)))
        small_p.append(ps)
        ps, tg = [], []
        m = c // 2
        while m >= SUBLANES:
            qs, ks, tgt = [], [], []
            for s0 in range(0, c, 2 * m):
                ref = b[s0 + m - 1:s0 + m, :]
                qs.append(q[s0 + m:s0 + 2 * m] * jnp.exp2(b[s0 + m:s0 + 2 * m] - ref))
                ks.append(k[s0:s0 + m] * jnp.exp2(ref - b[s0:s0 + m]))
                ks.append(k[s0 + m:s0 + 2 * m])
                tgt.extend(range((s0 + m) // SUBLANES, (s0 + 2 * m) // SUBLANES))
            ps.append(_dot_nt(jnp.concatenate(qs, 0).astype(BF16), jnp.concatenate(ks, 0).astype(BF16)))
            tg.append(tgt)
            m //= 2
        big_p.append(ps)
        big_tgt.append(tg)

    intra, qbs, upds, dcols = [], [], [], []
    for ci in range(nc):
        sl = pl.ds(ci * c, c)
        cols = slice(ci * HEAD_DIM, (ci + 1) * HEAD_DIM)
        b = e_all[0:c, cols]
        q = q_ref[sl, :].astype(F32)
        k = k_ref[sl, :].astype(F32)
        v = v_ref[sl, :]
        rows = [jnp.zeros((SUBLANES, c), F32) for _ in range(c // SUBLANES)]
        for li, p in enumerate(small_p[ci]):
            p = jnp.where(lev == li, p, 0.0)
            rows = [rw + p[i * SUBLANES:(i + 1) * SUBLANES] for i, rw in enumerate(rows)]
        for li, (p, tgt) in enumerate(zip(big_p[ci], big_tgt[ci])):
            p = p * big_ref[li]
            for n, i in enumerate(tgt):
                rows[i] = rows[i] + p[n * SUBLANES:(n + 1) * SUBLANES]
        scores = jnp.concatenate(rows, 0).astype(BF16)
        wb = jnp.exp2(b)
        we = jnp.exp2(b[c - 1:c, :] - b)
        intra.append(_dot(scores, v) + jnp.sum(q * k, axis=-1, keepdims=True) * v.astype(F32))
        qbs.append((q * wb).astype(BF16))
        upds.append(_dot_tn((k * we).astype(BF16), v))
        dcols.append(jnp.broadcast_to(wb[c - 1:c, :], (SUBLANES, HEAD_DIM)).T[:, :1])

    st = st_ref[...]
    states = []
    for ci in range(nc):
        states.append(st.astype(BF16))
        st = st * dcols[ci] + upds[ci]
    st_ref[...] = st

    for ci in range(nc):
        sl = pl.ds(ci * c, c)
        o = _dot(qbs[ci], states[ci]) + intra[ci]
        o_ref[sl, :] = (_rms(o, hg) * gt_ref[sl, :].astype(F32)).astype(BF16)


def _gla(q, k, v, logf, gate, head_gain, casts, batch, seq, ts):
    heads, m, _ = q.shape
    d = heads * HEAD_DIM
    ns = seq // ts
    dmat, lev, big = _gla_tables(GLA_CHUNK)
    hps = HEADS_PER_STEP
    hsteps = heads // hps
    spec = pl.BlockSpec((hps, ts, HEAD_DIM), lambda b, h, s: (h, b * ns + s, 0))
    const = lambda a: pl.BlockSpec(a.shape, lambda b, h, s: (0,) * a.ndim)
    c_in, c_out, c_shape, c_ops = _cast_plan(casts, batch * hsteps * ns, lambda b, h, s: (b * hsteps + h) * ns + s)
    outs = pl.pallas_call(
        functools.partial(_gla_kernel, GLA_CHUNK, len(casts)),
        grid=(batch, hsteps, ns),
        in_specs=[spec, spec, spec, spec, spec,
                  pl.BlockSpec((1, hps * HEAD_DIM), lambda b, h, s: (0, h)),
                  const(dmat), const(lev), const(big)] + c_in,
        out_specs=[spec] + c_out,
        out_shape=[jax.ShapeDtypeStruct((heads, m, HEAD_DIM), BF16)] + c_shape,
        scratch_shapes=[pltpu.VMEM((hps, HEAD_DIM, HEAD_DIM), F32)],
        compiler_params=_params("arbitrary", "arbitrary", "arbitrary"),
        name="gla",
    )(q, k, v, logf, gate, head_gain.reshape(1, d), jnp.asarray(dmat, BF16), jnp.asarray(lev),
      jnp.asarray(big), *c_ops)
    return outs[0], outs[1:]


def _proj_res_kernel(h_ref, a_ref, w_ref, o_ref):
    o_ref[...] = h_ref[...] + _dot(_load_heads(a_ref), w_ref[...])


def _resident_spec(w):
    return pl.BlockSpec(w.shape, lambda i: (0, 0), pipeline_mode=pl.Buffered(1))


def _proj_res(h, a, w, tm):
    m, d = h.shape
    return pl.pallas_call(
        _proj_res_kernel,
        grid=(m // tm,),
        in_specs=[pl.BlockSpec((tm, d), lambda i: (i, 0)),
                  pl.BlockSpec((a.shape[0], tm, HEAD_DIM), lambda i: (0, i, 0)),
                  _resident_spec(w)],
        out_specs=pl.BlockSpec((tm, d), lambda i: (i, 0)),
        out_shape=jax.ShapeDtypeStruct((m, d), F32),
        compiler_params=_params("arbitrary"),
        name="proj_res",
    )(h, a, w)


def _mlp_kernel(h_ref, ng_ref, w1_ref, w2_ref, o_ref, xn_ref):
    @pl.when(pl.program_id(1) == 0)
    def _():
        x = h_ref[...]
        xn_ref[...] = _rms(x, ng_ref[...]).astype(BF16)
        o_ref[...] = x

    t = jnp.square(jnp.maximum(_dot(xn_ref[...], w1_ref[...]), 0.0)).astype(BF16)
    o_ref[...] += _dot(t, w2_ref[...])


def _mlp(h, norm_g, w1, w2, tm, tf):
    m, d = h.shape
    ff = w1.shape[-1]
    return pl.pallas_call(
        _mlp_kernel,
        grid=(m // tm, ff // tf),
        in_specs=[pl.BlockSpec((tm, d), lambda i, j: (i, 0)),
                  pl.BlockSpec((1, d), lambda i, j: (0, 0)),
                  pl.BlockSpec((d, tf), lambda i, j: (0, j)),
                  pl.BlockSpec((tf, d), lambda i, j: (j, 0))],
        out_specs=pl.BlockSpec((tm, d), lambda i, j: (i, 0)),
        out_shape=jax.ShapeDtypeStruct((m, d), F32),
        scratch_shapes=[pltpu.VMEM((tm, d), BF16)],
        compiler_params=_params("arbitrary", "arbitrary"),
        name="mlp",
    )(h, norm_g.reshape(1, d), w1, w2)


def _norm_proj_kernel(n_normed, block_mean, out_scale, h_ref, ng_ref, hn_ref, w_ref, *outs):
    tm = h_ref.shape[0]
    sub = MOBA_BLOCK if tm % MOBA_BLOCK == 0 else tm
    nh = n_normed // HEAD_DIM
    for r0 in range(0, tm, sub):
        rows = slice(r0, r0 + sub)
        y = _dot(_rms(h_ref[rows, :], ng_ref[...]).astype(BF16), w_ref[...])
        for hh, yh in enumerate(_head_rms(y[:, :n_normed], hn_ref[...])):
            outs[0][hh, rows, :] = (yh if out_scale is None else yh * out_scale).astype(BF16)
            if block_mean:
                outs[-1][0, r0 // MOBA_BLOCK:(r0 + sub) // MOBA_BLOCK, hh * HEAD_DIM:(hh + 1) * HEAD_DIM] = (
                    jnp.sum(yh.reshape(sub // MOBA_BLOCK, MOBA_BLOCK, HEAD_DIM), axis=1) * (1.0 / MOBA_BLOCK))
        for hh in range(y.shape[1] // HEAD_DIM - nh):
            outs[1][hh, rows, :] = y[:, n_normed + hh * HEAD_DIM:n_normed + (hh + 1) * HEAD_DIM].astype(BF16)


def _norm_proj(h, norm_g, head_g, w, tm, n_normed, block_mean, name, out_scale=None):
    m, d = h.shape
    n = w.shape[1]
    nblk = tm // MOBA_BLOCK
    hm = lambda cols: (pl.BlockSpec((cols // HEAD_DIM, tm, HEAD_DIM), lambda i: (0, i, 0)),
                       jax.ShapeDtypeStruct((cols // HEAD_DIM, m, HEAD_DIM), BF16))
    outs = [hm(n_normed)] + ([hm(n - n_normed)] if n > n_normed else [])
    if block_mean:
        outs.append((pl.BlockSpec((1, nblk, n_normed), lambda i: (i, 0, 0)),
                     jax.ShapeDtypeStruct((m // tm, nblk, n_normed), F32)))
    return pl.pallas_call(
        functools.partial(_norm_proj_kernel, n_normed, block_mean, out_scale),
        grid=(m // tm,),
        in_specs=[pl.BlockSpec((tm, d), lambda i: (i, 0)),
                  pl.BlockSpec((1, d), lambda i: (0, 0)),
                  pl.BlockSpec((1, HEAD_DIM), lambda i: (0, 0)),
                  _resident_spec(w)],
        out_specs=[o[0] for o in outs],
        out_shape=[o[1] for o in outs],
        compiler_params=_params("arbitrary"),
        name=name,
    )(h, norm_g.reshape(1, d), head_g.reshape(1, HEAD_DIM), w)


def _moba_kernel(nblk, ncast, q_ref, k_ref, v_ref, km_ref, slope_ref, *refs):
    o_ref, vt_ref, bias_ref = refs[ncast], refs[-2], refs[-1]
    _run_casts(refs[:ncast], refs[ncast + 1:-2])
    for hd in range(q_ref.shape[0]):
        _moba_head(nblk, q_ref.at[hd], k_ref.at[hd], v_ref.at[hd],
                   km_ref[0][:, hd * HEAD_DIM:(hd + 1) * HEAD_DIM], slope_ref[hd], o_ref.at[hd], vt_ref, bias_ref)


def _moba_head(nblk, q_ref, k_ref, v_ref, kmean_f32, slope_row, o_ref, vt_ref, bias_ref):
    blk = MOBA_BLOCK
    grp = blk // SUBLANES
    for n in range(nblk):
        vt_ref[0:HEAD_DIM, n * blk:(n + 1) * blk] = v_ref[n * blk:(n + 1) * blk, :].astype(F32).T.astype(BF16)
    vt_ref[HEAD_DIM:, :] = jnp.ones((vt_ref.shape[0] - HEAD_DIM, vt_ref.shape[1]), BF16)
    slope = slope_row[:, :1]
    kmean = kmean_f32.astype(BF16)
    t_idx = lax.broadcasted_iota(jnp.int32, (blk, blk), 1)
    s_idx = lax.broadcasted_iota(jnp.int32, (blk, blk), 0)
    dist0 = (t_idx - s_idx).astype(F32)
    bias_ref[0] = jnp.where(dist0 >= 0.0, (LOG2E * slope) * dist0, -NEG_BIG)
    for dlt in range(1, nblk):
        bias_ref[dlt] = (LOG2E * slope) * (dist0 + float(dlt * blk))
    n_idx = lax.broadcasted_iota(jnp.int32, (nblk, blk), 0)

    def scores(i):
        qi = q_ref[i * blk:(i + 1) * blk, :]
        sel = None
        if i > MOBA_TOPK:
            gm = jnp.where(n_idx < i, _dot_nt(kmean, qi), NEG_BIG)
            rank = jnp.zeros((nblk, blk), F32)
            for mrow in range(nblk):
                gr = gm[mrow:mrow + 1, :]
                rank = rank + ((gr > gm) | ((gr == gm) & (mrow < n_idx))).astype(F32)
            sel = ((rank < MOBA_TOPK) & (n_idx < i)).astype(F32)
        return sel, _dot_nt(k_ref[0:(i + 1) * blk, :], qi)

    def softmax(i, sel, s_all):
        s = []
        for j in range(i + 1):
            sj = s_all[j * blk:(j + 1) * blk].reshape(grp, SUBLANES, blk) \
                - bias_ref[i - j].reshape(grp, SUBLANES, blk)
            if sel is not None and j < i:
                keep = jnp.broadcast_to(sel[j:j + 1, :], (SUBLANES, blk)) > 0.0
                sj = jnp.where(keep[None], sj, NEG_BIG)
            s.append(sj)
        m8 = functools.reduce(jnp.maximum, [jnp.max(sj, axis=0) for sj in s])
        m = jnp.broadcast_to(jnp.max(m8, axis=0, keepdims=True), (SUBLANES, blk))
        return jnp.concatenate([jnp.exp2(sj - m[None]).reshape(blk, blk).astype(BF16) for sj in s], axis=0)

    order = list(range(nblk - 1, -1, -1))
    depth = 4
    pend = [scores(i) for i in order[:depth]]
    for n, i in enumerate(order):
        cur = pend.pop(0)
        if n + depth < nblk:
            pend.append(scores(order[n + depth]))
        acc = _dot(vt_ref[:, 0:(i + 1) * blk], softmax(i, *cur))
        o_ref[i * blk:(i + 1) * blk, :] = (acc[0:HEAD_DIM] / acc[HEAD_DIM:HEAD_DIM + 1]).T.astype(BF16)


def _moba(q, k, v, kmean, casts, batch, seq):
    heads, m, _ = q.shape
    nblk = seq // MOBA_BLOCK
    hps = HEADS_PER_STEP
    hsteps = heads // hps
    c_in, c_out, c_shape, c_ops = _cast_plan(casts, batch * hsteps, lambda b, h: b * hsteps + h)
    slopes = 2.0 ** (-8.0 * jnp.arange(1, heads + 1, dtype=F32) / heads)
    slopes = jnp.broadcast_to(slopes[:, None, None], (heads, 1, HEAD_DIM))
    spec = pl.BlockSpec((hps, seq, HEAD_DIM), lambda b, h: (h, b, 0))
    outs = pl.pallas_call(
        functools.partial(_moba_kernel, nblk, len(casts)),
        grid=(batch, hsteps),
        in_specs=[spec, spec, spec,
                  pl.BlockSpec((1, nblk, hps * HEAD_DIM), lambda b, h: (b, 0, h)),
                  pl.BlockSpec((hps, 1, HEAD_DIM), lambda b, h: (h, 0, 0))] + c_in,
        out_specs=[spec] + c_out,
        out_shape=[jax.ShapeDtypeStruct((heads, m, HEAD_DIM), BF16)] + c_shape,
        scratch_shapes=[pltpu.VMEM((HEAD_DIM + 2 * SUBLANES, seq), BF16),
                        pltpu.VMEM((nblk, MOBA_BLOCK, MOBA_BLOCK), F32)],
        compiler_params=_params("arbitrary", "arbitrary"),
        name="moba_attn",
    )(q, k, v, kmean, slopes, *c_ops)
    return outs[0], outs[1:]


def _tile(n, pref):
    return pref if n % pref == 0 else n


def kernel(x, a_norm, a_w_in, a_head_norm, a_w_out, lower_bounds, kv_norm, w_kv, k_norm,
           b_norm, b_w_q, b_q_norm, b_w_o, mlp_norm, mlp_w1, mlp_w2):
    batch, seq, d = x.shape
    n_a = a_w_in.shape[0]
    n_b = b_w_q.shape[0]
    assert seq % MOBA_BLOCK == 0 and d % HEAD_DIM == 0
    assert n_a >= 1
    m = batch * seq
    tm = _tile(m, 1024)
    ts = _tile(seq, 2048)
    tr = _tile(m, 512)

    h = x.reshape(m, d)
    kb = vb = kmean = None
    w_in, w_q, w_kvb = a_w_in, None, None
    for l in range(n_a + n_b):
        casts = [(mlp_w1, l), (mlp_w2, l)]
        if l < n_a:
            q, logf, k, v, gate = _hgrn_in(h, a_norm[l], lower_bounds, w_in, l, tm, 256)
            casts.append((a_w_out, l))
            if l + 1 < n_a:
                casts.append((a_w_in, l + 1))
            elif n_b:
                casts += [(w_kv, 0), (b_w_q, 0)]
            a, wb = _gla(q, k, v, logf, gate, a_head_norm[l], casts, batch, seq, ts)
            if l + 1 < n_a:
                w_in = wb[3]
            elif n_b:
                w_kvb, w_q = wb[3], wb[4]
        else:
            j = l - n_a
            if kb is None:
                kb, vb, kmean = _norm_proj(h, kv_norm, k_norm, w_kvb, tr, d, True, "shared_kv")
                kmean = kmean.reshape(batch, seq // MOBA_BLOCK, d)
            q, = _norm_proj(h, b_norm[j], b_q_norm[j], w_q, tr, d, False, "moba_q",
                            out_scale=LOG2E * HEAD_DIM ** -0.5)
            casts.append((b_w_o, j))
            if j + 1 < n_b:
                casts.append((b_w_q, j + 1))
            a, wb = _moba(q, kb, vb, kmean, casts, batch, seq)
            if j + 1 < n_b:
                w_q = wb[3]
        h = _proj_res(h, a, wb[2], tm)
        h = _mlp(h, mlp_norm[l], wb[0], wb[1], tm, 1024)
    return h.reshape(batch, seq, d)
```
